```python
import math
import jax
import jax.numpy as jnp
from jax import lax
import numpy as np

D_MODEL = 2048
BATCH = 1
SEQ = 8192
DEPTH = 1
DEC_BATCH = 128
DEC_SEQ = 1
PAST_LEN = 2048
PAGE_SIZE = 128

HEAD_DIM = 128
N_SB_HEADS = 8
N_ML_HEADS = 8
SB_WIDTH = N_SB_HEADS * HEAD_DIM
ML_WIDTH = N_ML_HEADS * HEAD_DIM
MIX_WIDTH = SB_WIDTH + ML_WIDTH
D_FF = 4 * D_MODEL
PLE_DIM = 256
Q_BLOCK = 128
ML_CHUNK = 64
RMS_EPS = 1e-6
SB_BIAS_INIT = -8.0
QK_GAIN_INIT = 0.3
PROJ_SPLITS = (SB_WIDTH, SB_WIDTH, SB_WIDTH, ML_WIDTH, ML_WIDTH, ML_WIDTH, ML_WIDTH, N_ML_HEADS, N_ML_HEADS)
PROJ_WIDTH = 3 * SB_WIDTH + 4 * ML_WIDTH + 2 * N_ML_HEADS

kernel_name = "hybrid_stickbreak_mlstm_decoder_step"


def rms_norm(x, g):
    xf = x.astype(jnp.float32)
    y = xf * lax.rsqrt(jnp.mean(xf * xf, axis=-1, keepdims=True) + RMS_EPS)
    return (y * g.astype(jnp.float32)).astype(x.dtype)


def bhtd(t):
    return jnp.swapaxes(t, 1, 2)


def mixer_inputs(x, g_mix, w_in, b_gate, g_q, g_k):
    B, T, _ = x.shape
    proj = rms_norm(x, g_mix) @ w_in
    cuts = np.cumsum(PROJ_SPLITS)[:-1].tolist()
    q_sb, k_sb, v_sb, q_ml, k_ml, v_ml, o_ml, i_pre, f_pre = jnp.split(proj, cuts, axis=-1)

    def heads(t):
        return t.reshape(B, T, -1, HEAD_DIM)

    q_sb = rms_norm(heads(q_sb), g_q)
    k_sb = rms_norm(heads(k_sb), g_k)
    i_pre = (i_pre + b_gate[:N_ML_HEADS]).astype(jnp.float32)
    f_pre = (f_pre + b_gate[N_ML_HEADS:]).astype(jnp.float32)
    log_i = jnp.swapaxes(i_pre, 1, 2)
    log_f = jnp.swapaxes(jax.nn.log_sigmoid(f_pre), 1, 2)
    return q_sb, k_sb, heads(v_sb), heads(q_ml), heads(k_ml), heads(v_ml), heads(o_ml), log_i, log_f


def stick_breaking(q, k, v, q_pos, k_pos, b_sb):
    f32 = jnp.float32
    z = (jnp.einsum('bhqd,bhkd->bhqk', q.astype(f32), k.astype(f32)) * (HEAD_DIM ** -0.5)
         + b_sb.astype(f32)[None, :, None, None])
    causal = k_pos[None, :] < q_pos[:, None]
    log_beta = jax.nn.log_sigmoid(z)
    log_keep = jnp.where(causal, log_beta - z, 0.0)
    between = lax.cumsum(log_keep, axis=3, reverse=True) - log_keep
    a = jnp.where(causal, jnp.exp(log_beta + between), 0.0)
    return jnp.einsum('bhqk,bhkd->bhqd', a, v.astype(f32))


def stick_breaking_prompt(q, k, v, b_sb):
    B, H, T, d = q.shape
    nblk = T // Q_BLOCK
    qb = jnp.moveaxis(q.reshape(B, H, nblk, Q_BLOCK, d), 2, 0)
    k_pos = jnp.arange(T)

    def block(args):
        qi, i = args
        q_pos = i * Q_BLOCK + jnp.arange(Q_BLOCK)
        return stick_breaking(qi, k, v, q_pos, k_pos, b_sb)

    out = lax.map(block, (qb, jnp.arange(nblk)))
    return jnp.moveaxis(out, 0, 2).reshape(B, H, T, d)


def mlstm(q, k, v, log_i, log_f, C0, n0, m0):
    f32 = jnp.float32
    B, H, T, d = q.shape
    L = math.gcd(T, ML_CHUNK)
    nc = T // L

    def chunks(t):
        t = t.astype(f32)
        return jnp.moveaxis(t.reshape((B, H, nc, L) + t.shape[3:]), 2, 0)

    tril = jnp.tril(jnp.ones((L, L), dtype=bool))

    def step(carry, xs):
        C, n, m = carry
        qc, kc, vc, lic, lfc = xs
        b = jnp.cumsum(lfc, axis=-1)
        D = jnp.where(tril, b[..., :, None] - b[..., None, :] + lic[..., None, :], -jnp.inf)
        inter = b + m[..., None]
        m_row = jnp.maximum(inter, jnp.max(D, axis=-1))
        s = jnp.einsum('bhtd,bhsd->bhts', qc, kc) * jnp.exp(D - m_row[..., None])
        w_inter = jnp.exp(inter - m_row)
        num = jnp.einsum('bhts,bhsd->bhtd', s, vc) + w_inter[..., None] * jnp.einsum('bhvk,bhtk->bhtv', C, qc)
        den = jnp.sum(s, axis=-1) + w_inter * jnp.einsum('bhk,bhtk->bht', n, qc)
        h = num / jnp.maximum(jnp.abs(den), jnp.exp(-m_row))[..., None]
        g = b[..., -1:] - b + lic
        m_new = jnp.maximum(b[..., -1] + m, jnp.max(g, axis=-1))
        wk = jnp.exp(g - m_new[..., None])
        decay = jnp.exp(b[..., -1] + m - m_new)
        C_new = decay[..., None, None] * C + jnp.einsum('bhsv,bhsk->bhvk', wk[..., None] * vc, kc)
        n_new = decay[..., None] * n + jnp.einsum('bhs,bhsk->bhk', wk, kc)
        return (C_new, n_new, m_new), h

    k = k.astype(f32) * (d ** -0.5)
    (C, n, m), h = lax.scan(step, (C0.astype(f32), n0.astype(f32), m0.astype(f32)),
                            (chunks(q), chunks(k), chunks(v), chunks(log_i), chunks(log_f)))
    h = jnp.moveaxis(h, 0, 2).reshape(B, H, T, d)
    return h, C, n, m


def merge_and_channel(x, sb, hm, o_gate, g_mh, w_out, g_ffn, w_up, w_down, g_ple, w_ple, w_pg, p):
    B, T, _ = x.shape
    sb = bhtd(sb).reshape(B, T, SB_WIDTH)
    hm = (rms_norm(bhtd(hm), g_mh) * jax.nn.sigmoid(o_gate.astype(jnp.float32))).reshape(B, T, ML_WIDTH)
    mix = jnp.concatenate([sb, hm], axis=-1).astype(x.dtype)
    h = x + mix @ w_out
    u = rms_norm(h, g_ffn) @ w_up
    h = h + jnp.square(jax.nn.relu(u)) @ w_down
    gate = jax.nn.sigmoid(rms_norm(h, g_ple) @ w_pg)
    return h + (p @ w_ple) * gate


def setup_inputs(seed: int = 0) -> dict:
    key = jax.random.key(seed)
    ks = jax.random.split(key, 25)
    f32 = jnp.float32
    n_pages = PAST_LEN // PAGE_SIZE
    n_used = DEC_BATCH * n_pages
    n_pool = n_used + (n_used + 3) // 4

    def nrm(k, shape, scale=1.0):
        return jax.random.normal(k, shape, f32) * scale

    page_table = jax.random.permutation(ks[0], n_pool)[:n_used].reshape(DEC_BATCH, n_pages).astype(jnp.int32)
    forget_bias = jnp.linspace(3.0, 6.0, N_ML_HEADS, dtype=f32)[None, :] + nrm(ks[1], (DEPTH, N_ML_HEADS), 0.1)
    input_bias = nrm(ks[2], (DEPTH, N_ML_HEADS), 0.1)
    return {
        "x_prompt": nrm(ks[3], (BATCH, SEQ, D_MODEL)),
        "x_sample": nrm(ks[4], (DEC_BATCH, DEC_SEQ, D_MODEL)),
        "cache_k": nrm(ks[5], (DEPTH, n_pool, PAGE_SIZE, N_SB_HEADS, HEAD_DIM), QK_GAIN_INIT),
        "cache_v": nrm(ks[6], (DEPTH, n_pool, PAGE_SIZE, N_SB_HEADS, HEAD_DIM)),
        "state_C": nrm(ks[7], (DEPTH, DEC_BATCH, N_ML_HEADS, HEAD_DIM, HEAD_DIM), 0.1),
        "state_n": nrm(ks[8], (DEPTH, DEC_BATCH, N_ML_HEADS, HEAD_DIM), 0.5),
        "state_m": nrm(ks[9], (DEPTH, DEC_BATCH, N_ML_HEADS), 0.5),
        "page_table": page_table,
        "p_prompt": nrm(ks[10], (DEPTH, BATCH, SEQ, PLE_DIM)),
        "p_sample": nrm(ks[11], (DEPTH, DEC_BATCH, DEC_SEQ, PLE_DIM)),
        "g_mix": 1.0 + nrm(ks[12], (DEPTH, D_MODEL), 0.05),
        "w_in": nrm(ks[13], (DEPTH, D_MODEL, PROJ_WIDTH), D_MODEL ** -0.5),
        "b_gate": jnp.concatenate([input_bias, forget_bias], axis=-1),
        "b_sb": SB_BIAS_INIT + nrm(ks[24], (DEPTH, N_SB_HEADS), 0.1),
        "g_q": QK_GAIN_INIT * (1.0 + nrm(ks[14], (DEPTH, HEAD_DIM), 0.05)),
        "g_k": QK_GAIN_INIT * (1.0 + nrm(ks[15], (DEPTH, HEAD_DIM), 0.05)),
        "g_mh": 1.0 + nrm(ks[16], (DEPTH, HEAD_DIM), 0.05),
        "w_out": nrm(ks[17], (DEPTH, MIX_WIDTH, D_MODEL), MIX_WIDTH ** -0.5),
        "g_ffn": 1.0 + nrm(ks[18], (DEPTH, D_MODEL), 0.05),
        "w_up": nrm(ks[19], (DEPTH, D_MODEL, D_FF), D_MODEL ** -0.5),
        "w_down": nrm(ks[20], (DEPTH, D_FF, D_MODEL), D_FF ** -0.5),
        "g_ple": 1.0 + nrm(ks[21], (DEPTH, D_MODEL), 0.05),
        "w_ple": nrm(ks[22], (DEPTH, PLE_DIM, D_MODEL), PLE_DIM ** -0.5),
        "w_pg": nrm(ks[23], (DEPTH, D_MODEL, D_MODEL), D_MODEL ** -0.5),
    }


def reference(x_prompt, x_sample, cache_k, cache_v, state_C, state_n, state_m, page_table, p_prompt, p_sample,
              g_mix, w_in, b_gate, b_sb, g_q, g_k, g_mh, w_out, g_ffn, w_up, w_down, g_ple, w_ple, w_pg):
    B, T, _ = x_prompt.shape
    DB, TS, _ = x_sample.shape
    past = page_table.shape[1] * cache_k.shape[2]
    f32 = jnp.float32
    xp, xs = x_prompt, x_sample
    kp_l, vp_l, Cp_l, np_l, mp_l = [], [], [], [], []
    ks_l, vs_l, Cs_l, ns_l, ms_l = [], [], [], [], []
    for l in range(DEPTH):
        q, k, v, qm, km, vm, om, li, lf = mixer_inputs(xp, g_mix[l], w_in[l], b_gate[l], g_q[l], g_k[l])
        sb = stick_breaking_prompt(bhtd(q), bhtd(k), bhtd(v), b_sb[l])
        C0 = jnp.zeros((B, N_ML_HEADS, HEAD_DIM, HEAD_DIM), f32)
        n0 = jnp.zeros((B, N_ML_HEADS, HEAD_DIM), f32)
        m0 = jnp.zeros((B, N_ML_HEADS), f32)
        hm, C, n, m = mlstm(bhtd(qm), bhtd(km), bhtd(vm), li, lf, C0, n0, m0)
        xp = merge_and_channel(xp, sb, hm, om, g_mh[l], w_out[l], g_ffn[l], w_up[l], w_down[l],
                               g_ple[l], w_ple[l], w_pg[l], p_prompt[l])
        kp_l.append(k); vp_l.append(v); Cp_l.append(C); np_l.append(n); mp_l.append(m)

        q, k, v, qm, km, vm, om, li, lf = mixer_inputs(xs, g_mix[l], w_in[l], b_gate[l], g_q[l], g_k[l])
        k_past = cache_k[l][page_table].reshape(DB, past, N_SB_HEADS, HEAD_DIM)
        v_past = cache_v[l][page_table].reshape(DB, past, N_SB_HEADS, HEAD_DIM)
        k_all = jnp.concatenate([k_past, k.astype(k_past.dtype)], axis=1)
        v_all = jnp.concatenate([v_past, v.astype(v_past.dtype)], axis=1)
        q_pos = past + jnp.arange(TS)
        k_pos = jnp.arange(past + TS)
        sb = stick_breaking(bhtd(q), bhtd(k_all), bhtd(v_all), q_pos, k_pos, b_sb[l])
        hm, C, n, m = mlstm(bhtd(qm), bhtd(km), bhtd(vm), li, lf, state_C[l], state_n[l], state_m[l])
        xs = merge_and_channel(xs, sb, hm, om, g_mh[l], w_out[l], g_ffn[l], w_up[l], w_down[l],
                               g_ple[l], w_ple[l], w_pg[l], p_sample[l])
        ks_l.append(k); vs_l.append(v); Cs_l.append(C); ns_l.append(n); ms_l.append(m)

    return (xp, xs,
            jnp.stack(kp_l), jnp.stack(vp_l), jnp.stack(Cp_l), jnp.stack(np_l), jnp.stack(mp_l),
            jnp.stack(ks_l), jnp.stack(vs_l), jnp.stack(Cs_l), jnp.stack(ns_l), jnp.stack(ms_l))
```

```python
import functools
import math

import jax
import jax.numpy as jnp
from jax import lax
from jax.experimental import pallas as pl
from jax.experimental.pallas import tpu as pltpu

F32 = jnp.float32
BF16 = jnp.bfloat16

HEAD_DIM = 128
N_HEADS = 8
WIDTH = N_HEADS * HEAD_DIM
RMS_EPS = 1e-6
QK_SCALE = HEAD_DIM ** -0.5
LANES = 128
VMEM_LIMIT = 56 * 1024 * 1024

NT_DIMS = (((1,), (1,)), ((), ()))
TN_DIMS = (((0,), (0,)), ((), ()))


def _params(n_grid_dims):
    return pltpu.CompilerParams(
        dimension_semantics=("arbitrary",) * n_grid_dims,
        vmem_limit_bytes=VMEM_LIMIT)


def _log_sigmoid(x):
    return jnp.minimum(x, 0.0) - jnp.log1p(jnp.exp(-jnp.abs(x)))


def _sigmoid(x):
    return 1.0 / (1.0 + jnp.exp(-x))


def _split2(x):
    hi = x.astype(BF16)
    lo = (x - hi.astype(F32)).astype(BF16)
    return hi, lo


def _split3(x):
    x1 = x.astype(BF16)
    r1 = x - x1.astype(F32)
    x2 = r1.astype(BF16)
    x3 = (r1 - x2.astype(F32)).astype(BF16)
    return x1, x2, x3


def _dot(a, b):
    return jnp.dot(a, b, preferred_element_type=F32)


def _rms(x, g):
    return x * lax.rsqrt(jnp.mean(x * x, axis=-1, keepdims=True) + RMS_EPS) * g


def _norm_kernel(x_ref, g_ref, o_ref):
    o_ref[...] = _rms(x_ref[...], g_ref[...]).astype(o_ref.dtype)


def _norm_gate_kernel(x_ref, g_ref, wh_ref, wl_ref, b_ref, o_ref, gate_ref):
    y = _rms(x_ref[...], g_ref[...])
    yh, yl = _split2(y)
    o_ref[...] = yh
    wh = wh_ref[...]
    gate_ref[...] = _dot(yh, wh) + _dot(yh, wl_ref[...]) + _dot(yl, wh) + b_ref[...]


def _norm(x, g, tm):
    m, d = x.shape
    return pl.pallas_call(
        _norm_kernel,
        grid=(m // tm,),
        in_specs=[pl.BlockSpec((tm, d), lambda i: (i, 0)),
                  pl.BlockSpec((1, d), lambda i: (0, 0))],
        out_specs=pl.BlockSpec((tm, d), lambda i: (i, 0)),
        out_shape=jax.ShapeDtypeStruct((m, d), BF16),
        compiler_params=_params(1),
        name="rmsnorm",
    )(x, g.reshape(1, d))


def _norm_gate(x, g, wg_hi, wg_lo, bg, tm):
    m, d = x.shape
    return pl.pallas_call(
        _norm_gate_kernel,
        grid=(m // tm,),
        in_specs=[pl.BlockSpec((tm, d), lambda i: (i, 0)),
                  pl.BlockSpec((1, d), lambda i: (0, 0)),
                  pl.BlockSpec((d, LANES), lambda i: (0, 0)),
                  pl.BlockSpec((d, LANES), lambda i: (0, 0)),
                  pl.BlockSpec((1, LANES), lambda i: (0, 0))],
        out_specs=[pl.BlockSpec((tm, d), lambda i: (i, 0)),
                   pl.BlockSpec((tm, LANES), lambda i: (i, 0))],
        out_shape=[jax.ShapeDtypeStruct((m, d), BF16),
                   jax.ShapeDtypeStruct((m, LANES), F32)],
        compiler_params=_params(1),
        name="rmsnorm_gates",
    )(x, g.reshape(1, d), wg_hi, wg_lo, bg)


def _mm_kernel(*refs, n_a, n_e, epilogue):
    a_refs = refs[:n_a]
    w_refs = refs[n_a:2 * n_a]
    e_refs = refs[2 * n_a:2 * n_a + n_e]
    o_refs = refs[2 * n_a + n_e:]
    acc = None
    for a_ref, w_ref in zip(a_refs, w_refs):
        d = _dot(a_ref[...], w_ref[...])
        acc = d if acc is None else acc + d
    epilogue(acc, e_refs, o_refs)


def _matmul(name, a_list, w_list, extras, epilogue, out_dtypes, n, tm, tn):
    m = a_list[0].shape[0]
    in_specs, operands = [], []
    for a in a_list:
        in_specs.append(pl.BlockSpec((tm, a.shape[1]), lambda i, j: (i, 0)))
        operands.append(a)
    for w, k, rb, cb in w_list:
        in_specs.append(pl.BlockSpec((k, tn), lambda i, j, rb=rb, cb=cb: (rb, j + cb)))
        operands.append(w)
    for arr, blk, imap in extras:
        in_specs.append(pl.BlockSpec(blk, imap))
        operands.append(arr)
    out_specs = [pl.BlockSpec((tm, tn), lambda i, j: (i, j)) for _ in out_dtypes]
    out_shape = [jax.ShapeDtypeStruct((m, n), dt) for dt in out_dtypes]
    return pl.pallas_call(
        functools.partial(_mm_kernel, n_a=len(a_list), n_e=len(extras), epilogue=epilogue),
        grid=(m // tm, n // tn),
        in_specs=in_specs,
        out_specs=out_specs,
        out_shape=out_shape,
        compiler_params=_params(2),
        name=name,
    )(*operands)


def _ep_plain(acc, e_refs, o_refs):
    o_refs[0][...] = acc.astype(o_refs[0].dtype)


def _ep_headnorm(acc, e_refs, o_refs, *, scale):
    g = e_refs[0][...]
    for c in range(acc.shape[1] // HEAD_DIM):
        blk = acc[:, c * HEAD_DIM:(c + 1) * HEAD_DIM]
        y = _rms(blk, g)
        if scale != 1.0:
            y = y * scale
        o_refs[0][:, c * HEAD_DIM:(c + 1) * HEAD_DIM] = y.astype(o_refs[0].dtype)


def _ep_residual(acc, e_refs, o_refs):
    o_refs[0][...] = e_refs[0][...] + acc


def _ep_relu2(acc, e_refs, o_refs):
    r = jnp.maximum(acc, 0.0)
    o_refs[0][...] = (r * r).astype(o_refs[0].dtype)


def _ep_ple(acc, e_refs, o_refs):
    p_ref, wple_ref, h_ref = e_refs
    emb = _dot(p_ref[...].astype(BF16), wple_ref[...])
    o_refs[0][...] = h_ref[...] + emb * _sigmoid(acc)


def _sb_prompt_kernel(b_ref, q_ref, k_ref, v_ref, p_ref, o_ref, *, tq, tk):
    h = pl.program_id(0)
    i = pl.program_id(1)
    bias = b_ref[h]
    q = q_ref[...]
    paug = p_ref[...]
    row = lax.broadcasted_iota(jnp.int32, (tq, tk), 0)
    col = lax.broadcasted_iota(jnp.int32, (tq, tk), 1)
    n_diag = tq // tk

    def tile(j, c, acc, masked):
        start = pl.multiple_of(j * tk, tk)
        kt = k_ref[pl.ds(start, tk), :].astype(BF16)
        vt = v_ref[pl.ds(start, tk), :].astype(BF16)
        z = lax.dot_general(q, kt, NT_DIMS, preferred_element_type=F32) + bias
        sp = jnp.maximum(z, 0.0) + jnp.log1p(jnp.exp(-jnp.abs(z)))
        lk = -sp
        if masked:
            mask = (j * tk + col) < (i * tq + row)
            lk = jnp.where(mask, lk, 0.0)
        hi, lo = _split2(lk)
        r = _dot(hi, paug) + _dot(lo, paug)
        a = jnp.exp((z - sp) + r[:, :tk] + c)
        if masked:
            a = jnp.where(mask, a, 0.0)
        acc = acc + _dot(a.astype(BF16), vt)
        return c + r[:, tk:], acc

    c = jnp.zeros((tq, tk), F32)
    acc = jnp.zeros((tq, HEAD_DIM), F32)
    for d in range(n_diag):
        c, acc = tile(i * n_diag + (n_diag - 1 - d), c, acc, True)

    def body(t, carry):
        c, acc = carry
        return tile(i * n_diag - 1 - t, c, acc, False)

    c, acc = lax.fori_loop(0, i * n_diag, body, (c, acc))
    o_ref[...] = acc.astype(o_ref.dtype)


def _sb_prompt(q, k, v, b_sb, paug, tq, tk):
    t = q.shape[0]
    return pl.pallas_call(
        functools.partial(_sb_prompt_kernel, tq=tq, tk=tk),
        grid=(N_HEADS, t // tq),
        in_specs=[pl.BlockSpec(memory_space=pltpu.SMEM),
                  pl.BlockSpec((tq, HEAD_DIM), lambda h, i: (i, h)),
                  pl.BlockSpec((t, HEAD_DIM), lambda h, i: (0, h)),
                  pl.BlockSpec((t, HEAD_DIM), lambda h, i: (0, h)),
                  pl.BlockSpec((tk, 2 * tk), lambda h, i: (0, 0))],
        out_specs=pl.BlockSpec((tq, HEAD_DIM), lambda h, i: (i, h)),
        out_shape=jax.ShapeDtypeStruct((t, WIDTH), BF16),
        compiler_params=_params(2),
        name="sb_prompt",
    )(b_sb, q, k, v, paug)


def _sb_sample_kernel(pt_ref, q_ref, b_ref, k_ref, v_ref, p_ref, o_ref, qbd_ref, c_ref, acc_ref, *, n_pages):
    del pt_ref
    p = pl.program_id(1)

    @pl.when(p == 0)
    def _():
        q8 = q_ref[0] * QK_SCALE
        qpad = jnp.concatenate([q8, jnp.zeros((LANES - N_HEADS, HEAD_DIM), F32)], axis=0)
        qt = qpad.T
        lane = lax.broadcasted_iota(jnp.int32, (HEAD_DIM, LANES), 1)
        for h in range(N_HEADS):
            qbd_ref[h * HEAD_DIM:(h + 1) * HEAD_DIM, :] = jnp.where(lane == h, qt, 0.0).astype(BF16)
        c_ref[...] = jnp.zeros_like(c_ref)
        acc_ref[...] = jnp.zeros_like(acc_ref)

    kp = k_ref[0].astype(BF16)
    vp = v_ref[0].astype(BF16)
    z = _dot(kp, qbd_ref[...]) + b_ref[...]
    sp = jnp.maximum(z, 0.0) + jnp.log1p(jnp.exp(-jnp.abs(z)))
    hi, lo = _split2(-sp)
    paug = p_ref[...]
    r = _dot(paug, hi) + _dot(paug, lo)
    a = jnp.exp((z - sp) + r[:LANES, :] + c_ref[...])
    a8 = a[:, :N_HEADS].astype(BF16)
    acc_ref[...] += lax.dot_general(a8, vp, TN_DIMS, preferred_element_type=F32)
    c_ref[...] += r[LANES:, :]

    @pl.when(p == n_pages - 1)
    def _():
        acc = acc_ref[...]
        rowh = lax.broadcasted_iota(jnp.int32, acc.shape, 0)
        colh = lax.broadcasted_iota(jnp.int32, acc.shape, 1) // HEAD_DIM
        o_ref[0] = jnp.sum(jnp.where(rowh == colh, acc, 0.0), axis=0, keepdims=True).astype(o_ref.dtype)


def _sb_sample(q, b_row, cache_k, cache_v, page_table, paug):
    nb, n_pages = page_table.shape
    page = cache_k.shape[1]
    kv_spec = pl.BlockSpec((1, page, WIDTH), lambda b, p, pt: (pt[b, n_pages - 1 - p], 0, 0))
    grid_spec = pltpu.PrefetchScalarGridSpec(
        num_scalar_prefetch=1,
        grid=(nb, n_pages),
        in_specs=[pl.BlockSpec((1, N_HEADS, HEAD_DIM), lambda b, p, pt: (b, 0, 0)),
                  pl.BlockSpec((1, LANES), lambda b, p, pt: (0, 0)),
                  kv_spec, kv_spec,
                  pl.BlockSpec((2 * page, page), lambda b, p, pt: (0, 0))],
        out_specs=pl.BlockSpec((1, 1, WIDTH), lambda b, p, pt: (b, 0, 0)),
        scratch_shapes=[pltpu.VMEM((WIDTH, LANES), BF16),
                        pltpu.VMEM((page, LANES), F32),
                        pltpu.VMEM((N_HEADS, WIDTH), F32)])
    return pl.pallas_call(
        functools.partial(_sb_sample_kernel, n_pages=n_pages),
        grid_spec=grid_spec,
        out_shape=jax.ShapeDtypeStruct((nb, 1, WIDTH), BF16),
        compiler_params=_params(2),
        name="sb_sample",
    )(page_table, q, b_row, cache_k, cache_v, paug)


def _ml_prompt_kernel(q_ref, k_ref, v_ref, og_ref, g_ref, gt_ref, gmh_ref, tril_ref, triu_ref,
                      hm_ref, c_ref, n_ref, m_ref, *, chunk):
    step = pl.program_id(0)

    @pl.when(step == 0)
    def _():
        c_ref[...] = jnp.zeros_like(c_ref)
        n_ref[...] = jnp.zeros_like(n_ref)
        m_ref[...] = jnp.zeros_like(m_ref)

    g = g_ref[...]
    gt = gt_ref[...]
    lane = lax.broadcasted_iota(jnp.int32, g.shape, 1)
    rowi = lax.broadcasted_iota(jnp.int32, gt.shape, 0)
    gl = jnp.where(lane >= N_HEADS, _log_sigmoid(g), g)
    gtl = jnp.where(rowi >= N_HEADS, _log_sigmoid(gt), gt)
    tril = tril_ref[...]
    triu = triu_ref[...]
    a_col = sum(_dot(tril, x) for x in _split3(gl))
    a_row = sum(_dot(x, triu) for x in _split3(gtl))
    tt = lax.broadcasted_iota(jnp.int32, (chunk, chunk), 0)
    ss = lax.broadcasted_iota(jnp.int32, (chunk, chunk), 1)
    causal = ss <= tt
    gmh = gmh_ref[...]

    for h in range(N_HEADS):
        sl = slice(h * HEAD_DIM, (h + 1) * HEAD_DIM)
        b_col = a_col[:, N_HEADS + h:N_HEADS + h + 1]
        li_col = gl[:, h:h + 1]
        b_row = a_row[N_HEADS + h:N_HEADS + h + 1, :]
        li_row = gtl[h:h + 1, :]
        m_old = m_ref[h:h + 1, 0:1]
        b_end = b_col[chunk - 1:chunk, :]

        dmat = b_col - b_row + li_row
        inter = b_col + m_old
        m_row = jnp.maximum(inter, jnp.max(jnp.where(causal, dmat, -jnp.inf), axis=-1, keepdims=True))
        e = jnp.where(causal, jnp.exp(dmat - m_row), 0.0)
        w_inter = jnp.exp(inter - m_row)

        qc = q_ref[:, sl]
        kc = k_ref[:, sl] * QK_SCALE
        vc = v_ref[:, sl]
        qb = qc.astype(BF16)
        kb = kc.astype(BF16)
        s = lax.dot_general(qb, kb, NT_DIMS, preferred_element_type=F32) * e
        c_old = c_ref[h]
        n_old = n_ref[h:h + 1, :]
        cq = lax.dot_general(qb, c_old.astype(BF16), NT_DIMS, preferred_element_type=F32)
        num = _dot(s.astype(BF16), vc.astype(BF16)) + w_inter * cq
        den = jnp.sum(s, axis=-1, keepdims=True) + w_inter * jnp.sum(qc * n_old, axis=-1, keepdims=True)
        hh = num / jnp.maximum(jnp.abs(den), jnp.exp(-m_row))
        hm_ref[:, sl] = (_rms(hh, gmh) * _sigmoid(og_ref[:, sl])).astype(hm_ref.dtype)

        gw = b_end - b_col + li_col
        m_new = jnp.maximum(b_end + m_old, jnp.max(gw, axis=0, keepdims=True))
        wk = jnp.exp(gw - m_new)
        decay = jnp.exp(b_end + m_old - m_new)
        c_ref[h] = decay * c_old + lax.dot_general((wk * vc).astype(BF16), kb, TN_DIMS,
                                                   preferred_element_type=F32)
        n_ref[h:h + 1, :] = decay * n_old + jnp.sum(wk * kc, axis=0, keepdims=True)
        m_ref[h:h + 1, :] = jnp.broadcast_to(m_new, (1, LANES))


def _ml_prompt(ml, gates, gates_t, g_mh, chunk):
    t = ml.shape[0]
    tri = jnp.tril(jnp.ones((chunk, chunk), F32)).astype(BF16)
    col = lambda c: pl.BlockSpec((chunk, WIDTH), lambda s, c=c: (s, c))
    return pl.pallas_call(
        functools.partial(_ml_prompt_kernel, chunk=chunk),
        grid=(t // chunk,),
        in_specs=[col(0), col(1), col(2), col(3),
                  pl.BlockSpec((chunk, LANES), lambda s: (s, 0)),
                  pl.BlockSpec((2 * N_HEADS, chunk), lambda s: (0, s)),
                  pl.BlockSpec((1, HEAD_DIM), lambda s: (0, 0)),
                  pl.BlockSpec((chunk, chunk), lambda s: (0, 0)),
                  pl.BlockSpec((chunk, chunk), lambda s: (0, 0))],
        out_specs=[pl.BlockSpec((chunk, WIDTH), lambda s: (s, 0)),
                   pl.BlockSpec((N_HEADS, HEAD_DIM, HEAD_DIM), lambda s: (0, 0, 0)),
                   pl.BlockSpec((N_HEADS, HEAD_DIM), lambda s: (0, 0)),
                   pl.BlockSpec((N_HEADS, LANES), lambda s: (0, 0))],
        out_shape=[jax.ShapeDtypeStruct((t, WIDTH), BF16),
                   jax.ShapeDtypeStruct((N_HEADS, HEAD_DIM, HEAD_DIM), F32),
                   jax.ShapeDtypeStruct((N_HEADS, HEAD_DIM), F32),
                   jax.ShapeDtypeStruct((N_HEADS, LANES), F32)],
        compiler_params=_params(1),
        name="mlstm_prompt",
    )(ml, ml, ml, ml, gates, gates_t, g_mh.reshape(1, HEAD_DIM), tri, tri.T)


def _ml_sample_kernel(x_ref, g_ref, m_ref, c_ref, n_ref, gmh_ref, hm_ref, cn_ref, nn_ref, mn_ref):
    x = x_ref[0]
    q = x[0:N_HEADS]
    k = x[N_HEADS:2 * N_HEADS] * QK_SCALE
    v = x[2 * N_HEADS:3 * N_HEADS]
    og = x[3 * N_HEADS:4 * N_HEADS]
    gates = g_ref[0]
    li = gates[0:N_HEADS]
    lf = _log_sigmoid(gates[N_HEADS:2 * N_HEADS])
    m_old = m_ref[0]
    n_old = n_ref[0]

    inter = lf + m_old
    m_new = jnp.maximum(inter, li)
    s = jnp.sum(q * k, axis=-1, keepdims=True) * jnp.exp(li - m_new)
    w_inter = jnp.exp(inter - m_new)
    qb = q.astype(BF16)
    rowh = lax.broadcasted_iota(jnp.int32, (N_HEADS, HEAD_DIM), 0)
    cq = jnp.zeros((N_HEADS, HEAD_DIM), F32)
    for h in range(N_HEADS):
        r = lax.dot_general(qb, c_ref[0, h].astype(BF16), NT_DIMS, preferred_element_type=F32)
        cq = jnp.where(rowh == h, r, cq)
    num = s * v + w_inter * cq
    den = s + w_inter * jnp.sum(n_old * q, axis=-1, keepdims=True)
    hh = num / jnp.maximum(jnp.abs(den), jnp.exp(-m_new))
    hm_ref[0] = (_rms(hh, gmh_ref[...]) * _sigmoid(og)).astype(hm_ref.dtype)

    wk = jnp.exp(li - m_new)
    decay = w_inter
    nn_ref[0] = decay * n_old + wk * k
    mn_ref[0] = m_new
    wv = wk * v
    wvt = jnp.concatenate([wv, jnp.zeros((LANES - N_HEADS, HEAD_DIM), F32)], axis=0).T
    for h in range(N_HEADS):
        cn_ref[0, h] = decay[h:h + 1, :] * c_ref[0, h] + wvt[:, h:h + 1] * k[h:h + 1, :]


def _ml_sample(ml, gates, state_m, state_c, state_n, g_mh):
    nb = ml.shape[0]
    x = ml.reshape(nb, 4 * N_HEADS, HEAD_DIM)
    g = gates.reshape(nb, 2 * N_HEADS, 1)
    m = state_m.reshape(nb, N_HEADS, 1)
    b3 = lambda *shape: pl.BlockSpec((1,) + shape, lambda b: (b,) + (0,) * len(shape))
    return pl.pallas_call(
        _ml_sample_kernel,
        grid=(nb,),
        in_specs=[b3(4 * N_HEADS, HEAD_DIM), b3(2 * N_HEADS, 1), b3(N_HEADS, 1),
                  b3(N_HEADS, HEAD_DIM, HEAD_DIM), b3(N_HEADS, HEAD_DIM),
                  pl.BlockSpec((1, HEAD_DIM), lambda b: (0, 0))],
        out_specs=[b3(N_HEADS, HEAD_DIM), b3(N_HEADS, HEAD_DIM, HEAD_DIM), b3(N_HEADS, HEAD_DIM), b3(N_HEADS, 1)],
        out_shape=[jax.ShapeDtypeStruct((nb, N_HEADS, HEAD_DIM), BF16),
                   jax.ShapeDtypeStruct((nb, N_HEADS, HEAD_DIM, HEAD_DIM), F32),
                   jax.ShapeDtypeStruct((nb, N_HEADS, HEAD_DIM), F32),
                   jax.ShapeDtypeStruct((nb, N_HEADS, 1), F32)],
        compiler_params=_params(1),
        name="mlstm_sample",
    )(x, g, m, state_c, state_n, g_mh.reshape(1, HEAD_DIM))


def _dense_front(x, w, tm, q_scale, q_dtype):
    xn, gates = _norm_gate(x, w["g_mix"], w["wg_hi"], w["wg_lo"], w["bg"], min(tm, 512))
    d = x.shape[1]
    gq = (w["g_q"].reshape(1, HEAD_DIM), (1, HEAD_DIM), lambda i, j: (0, 0))
    gk = (w["g_k"].reshape(1, HEAD_DIM), (1, HEAD_DIM), lambda i, j: (0, 0))
    tn = 1024
    (q,) = _matmul("proj_q", [xn], [(w["w_in"], d, 0, 0)], [gq],
                   functools.partial(_ep_headnorm, scale=q_scale), [q_dtype], WIDTH, tm, tn)
    (k,) = _matmul("proj_k", [xn], [(w["w_in"], d, 0, WIDTH // tn)], [gk],
                   functools.partial(_ep_headnorm, scale=1.0), [F32], WIDTH, tm, tn)
    (v,) = _matmul("proj_v", [xn], [(w["w_in"], d, 0, 2 * WIDTH // tn)], [], _ep_plain, [F32], WIDTH, tm, tn)
    (ml,) = _matmul("proj_ml", [xn], [(w["w_in"], d, 0, 3 * WIDTH // tn)], [], _ep_plain, [F32], 4 * WIDTH, tm, tn)
    return q, k, v, ml, gates


def _dense_back(x, sb, hm, p, w, tm):
    m, d = x.shape
    tn = 1024
    res = lambda arr: (arr, (tm, tn), lambda i, j: (i, j))
    (h1,) = _matmul("out_proj", [sb, hm], [(w["w_out"], WIDTH, 0, 0), (w["w_out"], WIDTH, 1, 0)], [res(x)],
                    _ep_residual, [F32], d, tm, tn)
    hn = _norm(h1, w["g_ffn"], min(tm, 512))
    d_ff = w["w_up"].shape[1]
    (act,) = _matmul("ffn_up", [hn], [(w["w_up"], d, 0, 0)], [], _ep_relu2, [BF16], d_ff, tm, tn)
    tm2, tn2 = min(tm, 512), 512
    res2 = lambda arr: (arr, (tm2, tn2), lambda i, j: (i, j))
    (h2,) = _matmul("ffn_down", [act], [(w["w_down"], d_ff, 0, 0)], [res2(h1)], _ep_residual, [F32], d, tm2, tn2)
    hn2 = _norm(h2, w["g_ple"], min(tm, 512))
    ple = p.shape[1]
    extras = [(p, (tm, ple), lambda i, j: (i, 0)),
              (w["w_ple"], (ple, tn), lambda i, j: (0, j)),
              res(h2)]
    (y,) = _matmul("ple_gate", [hn2], [(w["w_pg"], d, 0, 0)], extras, _ep_ple, [F32], d, tm, tn)
    return y


def kernel(x_prompt, x_sample, cache_k, cache_v, state_C, state_n, state_m, page_table, p_prompt, p_sample,
           g_mix, w_in, b_gate, b_sb, g_q, g_k, g_mh, w_out, g_ffn, w_up, w_down, g_ple, w_ple, w_pg):
    depth = w_in.shape[0]
    assert depth == 1 and x_prompt.shape[0] == 1 and x_sample.shape[1] == 1
    t, d = x_prompt.shape[1], x_prompt.shape[2]
    nb = x_sample.shape[0]
    n_pool, page = cache_k.shape[1], cache_k.shape[2]
    n_proj = w_in.shape[2] - 2 * N_HEADS

    wg = jnp.pad(w_in[0, :, n_proj:], ((0, 0), (0, LANES - 2 * N_HEADS)))
    wg_hi = wg.astype(BF16)
    w = {
        "g_mix": g_mix[0], "g_q": g_q[0], "g_k": g_k[0], "g_ffn": g_ffn[0], "g_ple": g_ple[0],
        "wg_hi": wg_hi, "wg_lo": (wg - wg_hi.astype(F32)).astype(BF16),
        "bg": jnp.pad(b_gate[0], (0, LANES - 2 * N_HEADS)).reshape(1, LANES),
        "w_in": w_in[0].astype(BF16), "w_out": w_out[0].astype(BF16), "w_up": w_up[0].astype(BF16),
        "w_down": w_down[0].astype(BF16), "w_ple": w_ple[0].astype(BF16), "w_pg": w_pg[0].astype(BF16),
    }

    xp = x_prompt[0]
    q, k, v, ml, gates = _dense_front(xp, w, 1024, QK_SCALE, BF16)
    tk = 128
    jj = lax.broadcasted_iota(jnp.int32, (tk, 2 * tk), 0)
    sc = lax.broadcasted_iota(jnp.int32, (tk, 2 * tk), 1)
    paug = ((jj > sc) | (sc >= tk)).astype(BF16)
    sb = _sb_prompt(q, k, v, b_sb[0], paug, 256, tk)
    chunk = 128
    gates_t = gates[:, :2 * N_HEADS].T
    hm, c_p, n_p, m_p = _ml_prompt(ml, gates, gates_t, g_mh[0], chunk)
    y_p = _dense_back(xp, sb, hm, p_prompt[0, 0], w, 1024)

    xs = x_sample[:, 0]
    qs, ks, vs, mls, gates_s = _dense_front(xs, w, nb, 1.0, F32)
    b_row = jnp.pad(b_sb[0], (0, LANES - N_HEADS)).reshape(1, LANES)
    ck = cache_k[0].reshape(n_pool, page, WIDTH)
    cv = cache_v[0].reshape(n_pool, page, WIDTH)
    pj = lax.broadcasted_iota(jnp.int32, (2 * page, page), 1)
    pp = lax.broadcasted_iota(jnp.int32, (2 * page, page), 0)
    paug_s = ((pj > pp) | (pp >= page)).astype(BF16)
    sbs = _sb_sample(qs.reshape(nb, N_HEADS, HEAD_DIM), b_row, ck, cv, page_table, paug_s)
    hms, c_s, n_s, m_s = _ml_sample(mls, gates_s[:, :2 * N_HEADS], state_m[0], state_C[0], state_n[0], g_mh[0])
    y_s = _dense_back(xs, sbs.reshape(nb, WIDTH), hms.reshape(nb, WIDTH), p_sample[0, :, 0], w, nb)

    hd = (N_HEADS, HEAD_DIM)
    return (y_p[None], y_s[:, None],
            k.reshape((1, 1, t) + hd), v.reshape((1, 1, t) + hd),
            c_p[None, None], n_p[None, None], m_p[:, 0][None, None],
            ks.reshape((1, nb, 1) + hd), vs.reshape((1, nb, 1) + hd),
            c_s[None], n_s[None], m_s.reshape(1, nb, N_HEADS))
```

```python
import functools
import math

import jax
import jax.numpy as jnp
from jax import lax
from jax.experimental import pallas as pl
from jax.experimental.pallas import tpu as pltpu

F32 = jnp.float32
BF16 = jnp.bfloat16

HEAD_DIM = 128
N_HEADS = 8
WIDTH = N_HEADS * HEAD_DIM
RMS_EPS = 1e-6
QK_SCALE = HEAD_DIM ** -0.5
LANES = 128
VMEM_LIMIT = 56 * 1024 * 1024

NT_DIMS = (((1,), (1,)), ((), ()))
TN_DIMS = (((0,), (0,)), ((), ()))


def _params(n_grid_dims):
    return pltpu.CompilerParams(
        dimension_semantics=("arbitrary",) * n_grid_dims,
        vmem_limit_bytes=VMEM_LIMIT)


def _log_sigmoid(x):
    return jnp.minimum(x, 0.0) - jnp.log1p(jnp.exp(-jnp.abs(x)))


def _sigmoid(x):
    return 1.0 / (1.0 + jnp.exp(-x))


def _split2(x):
    hi = x.astype(BF16)
    lo = (x - hi.astype(F32)).astype(BF16)
    return hi, lo


def _split3(x):
    x1 = x.astype(BF16)
    r1 = x - x1.astype(F32)
    x2 = r1.astype(BF16)
    x3 = (r1 - x2.astype(F32)).astype(BF16)
    return x1, x2, x3


def _dot(a, b):
    return jnp.dot(a, b, preferred_element_type=F32)


def _rms(x, g):
    return x * lax.rsqrt(jnp.mean(x * x, axis=-1, keepdims=True) + RMS_EPS) * g


def _norm_kernel(x_ref, g_ref, o_ref):
    o_ref[...] = _rms(x_ref[...], g_ref[...]).astype(o_ref.dtype)


def _norm_gate_kernel(x_ref, g_ref, wh_ref, wl_ref, b_ref, o_ref, gate_ref):
    y = _rms(x_ref[...], g_ref[...])
    yh, yl = _split2(y)
    o_ref[...] = yh
    wh = wh_ref[...]
    gate_ref[...] = _dot(yh, wh) + _dot(yh, wl_ref[...]) + _dot(yl, wh) + b_ref[...]


def _norm(x, g, tm):
    m, d = x.shape
    return pl.pallas_call(
        _norm_kernel,
        grid=(m // tm,),
        in_specs=[pl.BlockSpec((tm, d), lambda i: (i, 0)),
                  pl.BlockSpec((1, d), lambda i: (0, 0))],
        out_specs=pl.BlockSpec((tm, d), lambda i: (i, 0)),
        out_shape=jax.ShapeDtypeStruct((m, d), BF16),
        compiler_params=_params(1),
        name="rmsnorm",
    )(x, g.reshape(1, d))


def _norm_gate(x, g, wg_hi, wg_lo, bg, tm):
    m, d = x.shape
    return pl.pallas_call(
        _norm_gate_kernel,
        grid=(m // tm,),
        in_specs=[pl.BlockSpec((tm, d), lambda i: (i, 0)),
                  pl.BlockSpec((1, d), lambda i: (0, 0)),
                  pl.BlockSpec((d, LANES), lambda i: (0, 0)),
                  pl.BlockSpec((d, LANES), lambda i: (0, 0)),
                  pl.BlockSpec((1, LANES), lambda i: (0, 0))],
        out_specs=[pl.BlockSpec((tm, d), lambda i: (i, 0)),
                   pl.BlockSpec((tm, LANES), lambda i: (i, 0))],
        out_shape=[jax.ShapeDtypeStruct((m, d), BF16),
                   jax.ShapeDtypeStruct((m, LANES), F32)],
        compiler_params=_params(1),
        name="rmsnorm_gates",
    )(x, g.reshape(1, d), wg_hi, wg_lo, bg)


def _mm_kernel(*refs, n_a, n_e, epilogue):
    a_refs = refs[:n_a]
    w_refs = refs[n_a:2 * n_a]
    e_refs = refs[2 * n_a:2 * n_a + n_e]
    o_refs = refs[2 * n_a + n_e:]
    acc = None
    for a_ref, w_ref in zip(a_refs, w_refs):
        d = _dot(a_ref[...], w_ref[...])
        acc = d if acc is None else acc + d
    epilogue(acc, e_refs, o_refs)


def _matmul(name, a_list, w_list, extras, epilogue, out_dtypes, n, tm, tn):
    m = a_list[0].shape[0]
    in_specs, operands = [], []
    for a in a_list:
        in_specs.append(pl.BlockSpec((tm, a.shape[1]), lambda i, j: (i, 0)))
        operands.append(a)
    for w, k, rb, cb in w_list:
        in_specs.append(pl.BlockSpec((k, tn), lambda i, j, rb=rb, cb=cb: (rb, j + cb)))
        operands.append(w)
    for arr, blk, imap in extras:
        in_specs.append(pl.BlockSpec(blk, imap))
        operands.append(arr)
    out_specs = [pl.BlockSpec((tm, tn), lambda i, j: (i, j)) for _ in out_dtypes]
    out_shape = [jax.ShapeDtypeStruct((m, n), dt) for dt in out_dtypes]
    return pl.pallas_call(
        functools.partial(_mm_kernel, n_a=len(a_list), n_e=len(extras), epilogue=epilogue),
        grid=(m // tm, n // tn),
        in_specs=in_specs,
        out_specs=out_specs,
        out_shape=out_shape,
        compiler_params=_params(2),
        name=name,
    )(*operands)


def _ep_plain(acc, e_refs, o_refs):
    for o_ref in o_refs:
        o_ref[...] = acc.astype(o_ref.dtype)


def _ep_headnorm(acc, e_refs, o_refs, *, scale):
    g = e_refs[0][...]
    for c in range(acc.shape[1] // HEAD_DIM):
        blk = acc[:, c * HEAD_DIM:(c + 1) * HEAD_DIM]
        y = _rms(blk, g)
        if scale != 1.0:
            y = y * scale
        for o_ref in o_refs:
            o_ref[:, c * HEAD_DIM:(c + 1) * HEAD_DIM] = y.astype(o_ref.dtype)


def _ep_residual(acc, e_refs, o_refs):
    o_refs[0][...] = e_refs[0][...] + acc


def _ep_relu2(acc, e_refs, o_refs):
    r = jnp.maximum(acc, 0.0)
    o_refs[0][...] = (r * r).astype(o_refs[0].dtype)


def _ep_ple(acc, e_refs, o_refs):
    p_ref, wple_ref, h_ref = e_refs
    emb = _dot(p_ref[...].astype(BF16), wple_ref[...])
    o_refs[0][...] = h_ref[...] + emb * _sigmoid(acc)


def _softplus(z):
    return jnp.maximum(z, 0.0) + jnp.log(1.0 + jnp.exp(-jnp.abs(z)))


def _split_trunc(x):
    hi_f = lax.bitcast_convert_type(lax.bitcast_convert_type(x, jnp.uint32) & jnp.uint32(0xFFFF0000), F32)
    return hi_f.astype(BF16), (x - hi_f).astype(BF16)


def _sb_prompt_kernel(b_ref, q_ref, k_ref, v_ref, p_ref, o_ref, *, tq):
    h = pl.program_id(0)
    i = pl.program_id(1)
    bias = b_ref[h]
    q = q_ref[...]
    p2 = p_ref[...]
    n_sub = tq // LANES

    def tile(j, c, acc, masked):
        start = pl.multiple_of(j * tq, tq)
        kt = k_ref[pl.ds(start, tq), :]
        vt = v_ref[pl.ds(start, tq), :]
        z = lax.dot_general(q, kt, NT_DIMS, preferred_element_type=F32) + bias
        sp = _softplus(z)
        if masked:
            mask = (lax.broadcasted_iota(jnp.int32, (tq, tq), 1)
                    < lax.broadcasted_iota(jnp.int32, (tq, tq), 0))
            sp = jnp.where(mask, sp, 0.0)
        hi, lo = _split_trunc(sp)
        t = z - sp
        es = [None] * n_sub
        for u in reversed(range(n_sub)):
            sl = slice(u * LANES, (u + 1) * LANES)
            r = _dot(jnp.concatenate([hi[:, sl], lo[:, sl]], axis=1), p2)
            es[u] = jnp.exp(t[:, sl] + r[:, :LANES] + c)
            c = c + r[:, LANES:]
        a = jnp.concatenate(es, axis=1)
        if masked:
            a = jnp.where(mask, a, 0.0)
        acc = acc + _dot(a.astype(BF16), vt)
        return c, acc

    c = jnp.zeros((tq, LANES), F32)
    acc = jnp.zeros((tq, HEAD_DIM), F32)
    c, acc = tile(i, c, acc, True)

    def body(s, carry):
        return tile(i - 1 - s, carry[0], carry[1], False)

    c, acc = lax.fori_loop(0, i, body, (c, acc))
    o_ref[...] = acc.astype(o_ref.dtype)


def _cumsum_matrix(group):
    lp = lax.broadcasted_iota(jnp.int32, (2 * LANES, 2 * LANES), 0) % LANES
    l = lax.broadcasted_iota(jnp.int32, (2 * LANES, 2 * LANES), 1)
    same = (lp % group) == (l % group)
    later = (lp > l) | (l >= LANES)
    return jnp.where(same & later, -1.0, 0.0).astype(BF16)


def _sb_prompt(q, k, v, b_sb, tq):
    t = q.shape[0]
    return pl.pallas_call(
        functools.partial(_sb_prompt_kernel, tq=tq),
        grid=(N_HEADS, t // tq),
        in_specs=[pl.BlockSpec(memory_space=pltpu.SMEM),
                  pl.BlockSpec((tq, HEAD_DIM), lambda h, i: (i, h)),
                  pl.BlockSpec((t, HEAD_DIM), lambda h, i: (0, h)),
                  pl.BlockSpec((t, HEAD_DIM), lambda h, i: (0, h)),
                  pl.BlockSpec((2 * LANES, 2 * LANES), lambda h, i: (0, 0))],
        out_specs=pl.BlockSpec((tq, HEAD_DIM), lambda h, i: (i, h)),
        out_shape=jax.ShapeDtypeStruct((t, WIDTH), BF16),
        compiler_params=_params(2),
        name="sb_prompt",
    )(b_sb, q, k, v, _cumsum_matrix(1))


def _sb_sample_kernel(pt_ref, q_ref, b_ref, p_ref, u_ref, *refs, n_pages):
    del pt_ref
    k_refs = refs[:n_pages]
    v_refs = refs[n_pages:2 * n_pages]
    o_ref, z_ref = refs[2 * n_pages:]
    rows = k_refs[0].shape[2] * N_HEADS
    n_tiles = rows // LANES

    q8 = q_ref[0] * QK_SCALE
    zero = jnp.zeros_like(q8)
    q16 = jnp.concatenate([jnp.concatenate([q8, zero], axis=1),
                           jnp.concatenate([zero, q8], axis=1)], axis=0).astype(BF16)
    diag = (lax.broadcasted_iota(jnp.int32, (N_HEADS, rows), 0)
            == lax.broadcasted_iota(jnp.int32, (N_HEADS, rows), 1) % N_HEADS)

    def pick(x):
        return jnp.sum(jnp.where(diag, x, 0.0), axis=0, keepdims=True)

    for pp in range(n_pages // 2):
        ka = k_refs[2 * pp][0, 0].reshape(rows, HEAD_DIM).astype(BF16)
        kb = k_refs[2 * pp + 1][0, 0].reshape(rows, HEAD_DIM).astype(BF16)
        res = lax.dot_general(q16, jnp.concatenate([ka, kb], axis=1), NT_DIMS, preferred_element_type=F32)
        z_ref[2 * pp:2 * pp + 1, :] = pick(res[:N_HEADS])
        z_ref[2 * pp + 1:2 * pp + 2, :] = pick(res[N_HEADS:])

    z = z_ref[...] + b_ref[...]
    sp = _softplus(z)
    hi, lo = _split_trunc(sp)
    p2 = p_ref[...]
    within, carry = [None] * n_tiles, [None] * n_tiles
    run = jnp.zeros((n_pages, LANES), F32)
    for c in reversed(range(n_tiles)):
        sl = slice(c * LANES, (c + 1) * LANES)
        r = _dot(jnp.concatenate([hi[:, sl], lo[:, sl]], axis=1), p2)
        within[c] = r[:, :LANES]
        carry[c] = run
        run = run + r[:, LANES:]
    later_pages = sum(_dot(u_ref[...], x) for x in _split3(run))
    t = z - sp
    a = jnp.concatenate([jnp.exp(t[:, c * LANES:(c + 1) * LANES] + within[c] + carry[c] + later_pages)
                         for c in range(n_tiles)], axis=1)

    acc = jnp.zeros((N_HEADS, HEAD_DIM), F32)
    for pp in range(n_pages // 2):
        sel = jnp.concatenate(
            [jnp.where(diag, jnp.broadcast_to(a[2 * pp:2 * pp + 1], (N_HEADS, rows)), 0.0),
             jnp.where(diag, jnp.broadcast_to(a[2 * pp + 1:2 * pp + 2], (N_HEADS, rows)), 0.0)],
            axis=0).astype(BF16)
        va = v_refs[2 * pp][0, 0].reshape(rows, HEAD_DIM).astype(BF16)
        vb = v_refs[2 * pp + 1][0, 0].reshape(rows, HEAD_DIM).astype(BF16)
        r = _dot(sel, jnp.concatenate([va, vb], axis=1))
        acc = acc + r[:N_HEADS, :HEAD_DIM] + r[N_HEADS:, HEAD_DIM:]
    o_ref[0] = acc.astype(o_ref.dtype)


def _sb_sample(q, b_sb, cache_k, cache_v, page_table):
    nb, n_pages = page_table.shape
    page = cache_k.shape[2]
    rows = page * N_HEADS
    b_row = jnp.tile(b_sb, page).reshape(1, rows)
    pi = lax.broadcasted_iota(jnp.int32, (n_pages, n_pages), 0)
    pj = lax.broadcasted_iota(jnp.int32, (n_pages, n_pages), 1)
    later = (pj > pi).astype(BF16)
    kv_specs = [pl.BlockSpec((1, 1, page, N_HEADS, HEAD_DIM), lambda b, pt, j=j: (0, pt[b, j], 0, 0, 0))
                for j in range(n_pages)]
    grid_spec = pltpu.PrefetchScalarGridSpec(
        num_scalar_prefetch=1,
        grid=(nb,),
        in_specs=[pl.BlockSpec((1, N_HEADS, HEAD_DIM), lambda b, pt: (b, 0, 0)),
                  pl.BlockSpec((1, rows), lambda b, pt: (0, 0)),
                  pl.BlockSpec((2 * LANES, 2 * LANES), lambda b, pt: (0, 0)),
                  pl.BlockSpec((n_pages, n_pages), lambda b, pt: (0, 0))] + kv_specs + kv_specs,
        out_specs=pl.BlockSpec((1, N_HEADS, HEAD_DIM), lambda b, pt: (b, 0, 0)),
        scratch_shapes=[pltpu.VMEM((n_pages, rows), F32)])
    return pl.pallas_call(
        functools.partial(_sb_sample_kernel, n_pages=n_pages),
        grid_spec=grid_spec,
        out_shape=jax.ShapeDtypeStruct((nb, N_HEADS, HEAD_DIM), BF16),
        compiler_params=_params(1),
        name="sb_sample",
    )(page_table, q, b_row, _cumsum_matrix(N_HEADS), later, *([cache_k] * n_pages), *([cache_v] * n_pages))


def _ml_prompt_kernel(q_ref, k_ref, v_ref, og_ref, g_ref, gt_ref, gmh_ref, tril_ref, triu_ref,
                      hm_ref, c_ref, n_ref, m_ref, *, chunk):
    step = pl.program_id(0)

    @pl.when(step == 0)
    def _():
        c_ref[...] = jnp.zeros_like(c_ref)
        n_ref[...] = jnp.zeros_like(n_ref)
        m_ref[...] = jnp.zeros_like(m_ref)

    g = g_ref[...]
    gt = gt_ref[...]
    lane = lax.broadcasted_iota(jnp.int32, g.shape, 1)
    rowi = lax.broadcasted_iota(jnp.int32, gt.shape, 0)
    gl = jnp.where(lane >= N_HEADS, _log_sigmoid(g), g)
    gtl = jnp.where(rowi >= N_HEADS, _log_sigmoid(gt), gt)
    tril = tril_ref[...]
    triu = triu_ref[...]
    a_col = sum(_dot(tril, x) for x in _split3(gl))
    a_row = sum(_dot(x, triu) for x in _split3(gtl))
    tt = lax.broadcasted_iota(jnp.int32, (chunk, chunk), 0)
    ss = lax.broadcasted_iota(jnp.int32, (chunk, chunk), 1)
    causal = ss <= tt
    gmh = gmh_ref[...]

    for h in range(N_HEADS):
        sl = slice(h * HEAD_DIM, (h + 1) * HEAD_DIM)
        b_col = a_col[:, N_HEADS + h:N_HEADS + h + 1]
        li_col = gl[:, h:h + 1]
        b_row = a_row[N_HEADS + h:N_HEADS + h + 1, :]
        li_row = gtl[h:h + 1, :]
        m_old = m_ref[h:h + 1, 0:1]
        b_end = b_col[chunk - 1:chunk, :]

        dmat = b_col - b_row + li_row
        inter = b_col + m_old
        m_row = jnp.maximum(inter, jnp.max(jnp.where(causal, dmat, -jnp.inf), axis=-1, keepdims=True))
        e = jnp.where(causal, jnp.exp(dmat - m_row), 0.0)
        w_inter = jnp.exp(inter - m_row)

        qc = q_ref[:, sl]
        kc = k_ref[:, sl] * QK_SCALE
        vc = v_ref[:, sl]
        qb = qc.astype(BF16)
        kb = kc.astype(BF16)
        s = lax.dot_general(qb, kb, NT_DIMS, preferred_element_type=F32) * e
        c_old = c_ref[h]
        n_old = n_ref[h:h + 1, :]
        cq = lax.dot_general(qb, c_old.astype(BF16), NT_DIMS, preferred_element_type=F32)
        num = _dot(s.astype(BF16), vc.astype(BF16)) + w_inter * cq
        den = jnp.sum(s, axis=-1, keepdims=True) + w_inter * jnp.sum(qc * n_old, axis=-1, keepdims=True)
        hh = num / jnp.maximum(jnp.abs(den), jnp.exp(-m_row))
        hm_ref[:, sl] = (_rms(hh, gmh) * _sigmoid(og_ref[:, sl])).astype(hm_ref.dtype)

        gw = b_end - b_col + li_col
        m_new = jnp.maximum(b_end + m_old, jnp.max(gw, axis=0, keepdims=True))
        wk = jnp.exp(gw - m_new)
        decay = jnp.exp(b_end + m_old - m_new)
        c_ref[h] = decay * c_old + lax.dot_general((wk * vc).astype(BF16), kb, TN_DIMS,
                                                   preferred_element_type=F32)
        n_ref[h:h + 1, :] = decay * n_old + jnp.sum(wk * kc, axis=0, keepdims=True)
        m_ref[h:h + 1, :] = jnp.broadcast_to(m_new, (1, LANES))


def _ml_prompt(ml, gates, gates_t, g_mh, chunk):
    t = ml.shape[0]
    tri = jnp.tril(jnp.ones((chunk, chunk), F32)).astype(BF16)
    col = lambda c: pl.BlockSpec((chunk, WIDTH), lambda s, c=c: (s, c))
    return pl.pallas_call(
        functools.partial(_ml_prompt_kernel, chunk=chunk),
        grid=(t // chunk,),
        in_specs=[col(0), col(1), col(2), col(3),
                  pl.BlockSpec((chunk, LANES), lambda s: (s, 0)),
                  pl.BlockSpec((2 * N_HEADS, chunk), lambda s: (0, s)),
                  pl.BlockSpec((1, HEAD_DIM), lambda s: (0, 0)),
                  pl.BlockSpec((chunk, chunk), lambda s: (0, 0)),
                  pl.BlockSpec((chunk, chunk), lambda s: (0, 0))],
        out_specs=[pl.BlockSpec((chunk, WIDTH), lambda s: (s, 0)),
                   pl.BlockSpec((N_HEADS, HEAD_DIM, HEAD_DIM), lambda s: (0, 0, 0)),
                   pl.BlockSpec((N_HEADS, HEAD_DIM), lambda s: (0, 0)),
                   pl.BlockSpec((N_HEADS, LANES), lambda s: (0, 0))],
        out_shape=[jax.ShapeDtypeStruct((t, WIDTH), BF16),
                   jax.ShapeDtypeStruct((N_HEADS, HEAD_DIM, HEAD_DIM), F32),
                   jax.ShapeDtypeStruct((N_HEADS, HEAD_DIM), F32),
                   jax.ShapeDtypeStruct((N_HEADS, LANES), F32)],
        compiler_params=_params(1),
        name="mlstm_prompt",
    )(ml, ml, ml, ml, gates, gates_t, g_mh.reshape(1, HEAD_DIM), tri, tri.T)


def _ml_sample_kernel(x_ref, g_ref, m_ref, c_ref, n_ref, gmh_ref, hm_ref, cn_ref, nn_ref, mn_ref):
    x = x_ref[0]
    q = x[0:N_HEADS]
    k = x[N_HEADS:2 * N_HEADS] * QK_SCALE
    v = x[2 * N_HEADS:3 * N_HEADS]
    og = x[3 * N_HEADS:4 * N_HEADS]
    gates = g_ref[0]
    li = gates[0:N_HEADS]
    lf = _log_sigmoid(gates[N_HEADS:2 * N_HEADS])
    m_old = m_ref[0]
    n_old = n_ref[0]

    inter = lf + m_old
    m_new = jnp.maximum(inter, li)
    s = jnp.sum(q * k, axis=-1, keepdims=True) * jnp.exp(li - m_new)
    w_inter = jnp.exp(inter - m_new)
    qb = q.astype(BF16)
    rowh = lax.broadcasted_iota(jnp.int32, (N_HEADS, HEAD_DIM), 0)
    cq = jnp.zeros((N_HEADS, HEAD_DIM), F32)
    for h in range(N_HEADS):
        r = lax.dot_general(qb, c_ref[0, h].astype(BF16), NT_DIMS, preferred_element_type=F32)
        cq = jnp.where(rowh == h, r, cq)
    num = s * v + w_inter * cq
    den = s + w_inter * jnp.sum(n_old * q, axis=-1, keepdims=True)
    hh = num / jnp.maximum(jnp.abs(den), jnp.exp(-m_new))
    hm_ref[0] = (_rms(hh, gmh_ref[...]) * _sigmoid(og)).astype(hm_ref.dtype)

    wk = jnp.exp(li - m_new)
    decay = w_inter
    nn_ref[0] = decay * n_old + wk * k
    mn_ref[0] = m_new
    wv = wk * v
    wvt = jnp.concatenate([wv, jnp.zeros((LANES - N_HEADS, HEAD_DIM), F32)], axis=0).T
    for h in range(N_HEADS):
        cn_ref[0, h] = decay[h:h + 1, :] * c_ref[0, h] + wvt[:, h:h + 1] * k[h:h + 1, :]


def _ml_sample(ml, gates, state_m, state_c, state_n, g_mh):
    nb = ml.shape[0]
    x = ml.reshape(nb, 4 * N_HEADS, HEAD_DIM)
    g = gates.reshape(nb, 2 * N_HEADS, 1)
    m = state_m.reshape(nb, N_HEADS, 1)
    b3 = lambda *shape: pl.BlockSpec((1,) + shape, lambda b: (b,) + (0,) * len(shape))
    return pl.pallas_call(
        _ml_sample_kernel,
        grid=(nb,),
        in_specs=[b3(4 * N_HEADS, HEAD_DIM), b3(2 * N_HEADS, 1), b3(N_HEADS, 1),
                  b3(N_HEADS, HEAD_DIM, HEAD_DIM), b3(N_HEADS, HEAD_DIM),
                  pl.BlockSpec((1, HEAD_DIM), lambda b: (0, 0))],
        out_specs=[b3(N_HEADS, HEAD_DIM), b3(N_HEADS, HEAD_DIM, HEAD_DIM), b3(N_HEADS, HEAD_DIM), b3(N_HEADS, 1)],
        out_shape=[jax.ShapeDtypeStruct((nb, N_HEADS, HEAD_DIM), BF16),
                   jax.ShapeDtypeStruct((nb, N_HEADS, HEAD_DIM, HEAD_DIM), F32),
                   jax.ShapeDtypeStruct((nb, N_HEADS, HEAD_DIM), F32),
                   jax.ShapeDtypeStruct((nb, N_HEADS, 1), F32)],
        compiler_params=_params(1),
        name="mlstm_sample",
    )(x, g, m, state_c, state_n, g_mh.reshape(1, HEAD_DIM))


def _dense_front(x, w, tm, q_scale, q_dtype, kv_dtypes):
    xn, gates = _norm_gate(x, w["g_mix"], w["wg_hi"], w["wg_lo"], w["bg"], min(tm, 512))
    d = x.shape[1]
    gq = (w["g_q"].reshape(1, HEAD_DIM), (1, HEAD_DIM), lambda i, j: (0, 0))
    gk = (w["g_k"].reshape(1, HEAD_DIM), (1, HEAD_DIM), lambda i, j: (0, 0))
    tn = 1024
    (q,) = _matmul("proj_q", [xn], [(w["w_in"], d, 0, 0)], [gq],
                   functools.partial(_ep_headnorm, scale=q_scale), [q_dtype], WIDTH, tm, tn)
    k = _matmul("proj_k", [xn], [(w["w_in"], d, 0, WIDTH // tn)], [gk],
                functools.partial(_ep_headnorm, scale=1.0), kv_dtypes, WIDTH, tm, tn)
    v = _matmul("proj_v", [xn], [(w["w_in"], d, 0, 2 * WIDTH // tn)], [], _ep_plain, kv_dtypes, WIDTH, tm, tn)
    (ml,) = _matmul("proj_ml", [xn], [(w["w_in"], d, 0, 3 * WIDTH // tn)], [], _ep_plain, [F32], 4 * WIDTH, tm, tn)
    return q, k, v, ml, gates


def _dense_back(x, sb, hm, p, w, tm):
    m, d = x.shape
    tn = 1024
    res = lambda arr: (arr, (tm, tn), lambda i, j: (i, j))
    (h1,) = _matmul("out_proj", [sb, hm], [(w["w_out"], WIDTH, 0, 0), (w["w_out"], WIDTH, 1, 0)], [res(x)],
                    _ep_residual, [F32], d, tm, tn)
    hn = _norm(h1, w["g_ffn"], min(tm, 512))
    d_ff = w["w_up"].shape[1]
    (act,) = _matmul("ffn_up", [hn], [(w["w_up"], d, 0, 0)], [], _ep_relu2, [BF16], d_ff, tm, tn)
    tm2, tn2 = min(tm, 512), 512
    res2 = lambda arr: (arr, (tm2, tn2), lambda i, j: (i, j))
    (h2,) = _matmul("ffn_down", [act], [(w["w_down"], d_ff, 0, 0)], [res2(h1)], _ep_residual, [F32], d, tm2, tn2)
    hn2 = _norm(h2, w["g_ple"], min(tm, 512))
    ple = p.shape[1]
    extras = [(p, (tm, ple), lambda i, j: (i, 0)),
              (w["w_ple"], (ple, tn), lambda i, j: (0, j)),
              res(h2)]
    (y,) = _matmul("ple_gate", [hn2], [(w["w_pg"], d, 0, 0)], extras, _ep_ple, [F32], d, tm, tn)
    return y


def kernel(x_prompt, x_sample, cache_k, cache_v, state_C, state_n, state_m, page_table, p_prompt, p_sample,
           g_mix, w_in, b_gate, b_sb, g_q, g_k, g_mh, w_out, g_ffn, w_up, w_down, g_ple, w_ple, w_pg):
    depth = w_in.shape[0]
    assert depth == 1 and x_prompt.shape[0] == 1 and x_sample.shape[1] == 1
    t, d = x_prompt.shape[1], x_prompt.shape[2]
    nb = x_sample.shape[0]
    n_proj = w_in.shape[2] - 2 * N_HEADS

    wg = jnp.pad(w_in[0, :, n_proj:], ((0, 0), (0, LANES - 2 * N_HEADS)))
    wg_hi = wg.astype(BF16)
    w = {
        "g_mix": g_mix[0], "g_q": g_q[0], "g_k": g_k[0], "g_ffn": g_ffn[0], "g_ple": g_ple[0],
        "wg_hi": wg_hi, "wg_lo": (wg - wg_hi.astype(F32)).astype(BF16),
        "bg": jnp.pad(b_gate[0], (0, LANES - 2 * N_HEADS)).reshape(1, LANES),
        "w_in": w_in[0].astype(BF16), "w_out": w_out[0].astype(BF16), "w_up": w_up[0].astype(BF16),
        "w_down": w_down[0].astype(BF16), "w_ple": w_ple[0].astype(BF16), "w_pg": w_pg[0].astype(BF16),
    }

    xp = x_prompt[0]
    q, (k, k_bf), (v, v_bf), ml, gates = _dense_front(xp, w, 1024, QK_SCALE, BF16, [F32, BF16])
    sb = _sb_prompt(q, k_bf, v_bf, b_sb[0], 512)
    chunk = 128
    gates_t = gates[:, :2 * N_HEADS].T
    hm, c_p, n_p, m_p = _ml_prompt(ml, gates, gates_t, g_mh[0], chunk)
    y_p = _dense_back(xp, sb, hm, p_prompt[0, 0], w, 1024)

    xs = x_sample[:, 0]
    qs, (ks,), (vs,), mls, gates_s = _dense_front(xs, w, nb, 1.0, F32, [F32])
    sbs = _sb_sample(qs.reshape(nb, N_HEADS, HEAD_DIM), b_sb[0], cache_k, cache_v, page_table)
    hms, c_s, n_s, m_s = _ml_sample(mls, gates_s[:, :2 * N_HEADS], state_m[0], state_C[0], state_n[0], g_mh[0])
    y_s = _dense_back(xs, sbs.reshape(nb, WIDTH), hms.reshape(nb, WIDTH), p_sample[0, :, 0], w, nb)

    hd = (N_HEADS, HEAD_DIM)
    return (y_p[None], y_s[:, None],
            k.reshape((1, 1, t) + hd), v.reshape((1, 1, t) + hd),
            c_p[None, None], n_p[None, None], m_p[:, 0][None, None],
            ks.reshape((1, nb, 1) + hd), vs.reshape((1, nb, 1) + hd),
            c_s[None], n_s[None], m_s.reshape(1, nb, N_HEADS))
```

```python
import functools
import math

import jax
import jax.numpy as jnp
from jax import lax
from jax.experimental import pallas as pl
from jax.experimental.pallas import tpu as pltpu

F32 = jnp.float32
BF16 = jnp.bfloat16

HEAD_DIM = 128
N_HEADS = 8
WIDTH = N_HEADS * HEAD_DIM
RMS_EPS = 1e-6
QK_SCALE = HEAD_DIM ** -0.5
LOG2E = 1.0 / math.log(2.0)
LANES = 128
VMEM_LIMIT = 56 * 1024 * 1024

NT_DIMS = (((1,), (1,)), ((), ()))
TN_DIMS = (((0,), (0,)), ((), ()))


def _params(n_grid_dims):
    return pltpu.CompilerParams(
        dimension_semantics=("arbitrary",) * n_grid_dims,
        vmem_limit_bytes=VMEM_LIMIT)


def _log_sigmoid(x):
    return jnp.minimum(x, 0.0) - jnp.log1p(jnp.exp(-jnp.abs(x)))


def _sigmoid(x):
    return 1.0 / (1.0 + jnp.exp(-x))


def _split2(x):
    hi = x.astype(BF16)
    lo = (x - hi.astype(F32)).astype(BF16)
    return hi, lo


def _split3(x):
    x1 = x.astype(BF16)
    r1 = x - x1.astype(F32)
    x2 = r1.astype(BF16)
    x3 = (r1 - x2.astype(F32)).astype(BF16)
    return x1, x2, x3


def _dot(a, b):
    return jnp.dot(a, b, preferred_element_type=F32)


def _rms(x, g):
    return x * lax.rsqrt(jnp.mean(x * x, axis=-1, keepdims=True) + RMS_EPS) * g


def _norm_kernel(x_ref, g_ref, o_ref):
    o_ref[...] = _rms(x_ref[...], g_ref[...]).astype(o_ref.dtype)


def _norm_gate_kernel(x_ref, g_ref, wh_ref, wl_ref, b_ref, o_ref, gate_ref):
    y = _rms(x_ref[...], g_ref[...])
    yh, yl = _split2(y)
    o_ref[...] = yh
    wh = wh_ref[...]
    gate_ref[...] = _dot(yh, wh) + _dot(yh, wl_ref[...]) + _dot(yl, wh) + b_ref[...]


def _norm(x, g, tm):
    m, d = x.shape
    return pl.pallas_call(
        _norm_kernel,
        grid=(m // tm,),
        in_specs=[pl.BlockSpec((tm, d), lambda i: (i, 0)),
                  pl.BlockSpec((1, d), lambda i: (0, 0))],
        out_specs=pl.BlockSpec((tm, d), lambda i: (i, 0)),
        out_shape=jax.ShapeDtypeStruct((m, d), BF16),
        compiler_params=_params(1),
        name="rmsnorm",
    )(x, g.reshape(1, d))


def _norm_gate(x, g, wg_hi, wg_lo, bg, tm):
    m, d = x.shape
    return pl.pallas_call(
        _norm_gate_kernel,
        grid=(m // tm,),
        in_specs=[pl.BlockSpec((tm, d), lambda i: (i, 0)),
                  pl.BlockSpec((1, d), lambda i: (0, 0)),
                  pl.BlockSpec((d, LANES), lambda i: (0, 0)),
                  pl.BlockSpec((d, LANES), lambda i: (0, 0)),
                  pl.BlockSpec((1, LANES), lambda i: (0, 0))],
        out_specs=[pl.BlockSpec((tm, d), lambda i: (i, 0)),
                   pl.BlockSpec((tm, LANES), lambda i: (i, 0))],
        out_shape=[jax.ShapeDtypeStruct((m, d), BF16),
                   jax.ShapeDtypeStruct((m, LANES), F32)],
        compiler_params=_params(1),
        name="rmsnorm_gates",
    )(x, g.reshape(1, d), wg_hi, wg_lo, bg)


def _mm_kernel(*refs, n_a, n_e, epilogue, w_transposed):
    a_refs = refs[:n_a]
    w_refs = refs[n_a:2 * n_a]
    e_refs = refs[2 * n_a:2 * n_a + n_e]
    o_refs = refs[2 * n_a + n_e:]
    acc = None
    for a_ref, w_ref in zip(a_refs, w_refs):
        if w_transposed:
            d = lax.dot_general(a_ref[...], w_ref[...], NT_DIMS, preferred_element_type=F32)
        else:
            d = _dot(a_ref[...], w_ref[...])
        acc = d if acc is None else acc + d
    epilogue(acc, e_refs, o_refs)


def _matmul(name, a_list, w_list, extras, epilogue, out_dtypes, n, tm, tn, w_transposed=False):
    m = a_list[0].shape[0]
    in_specs, operands = [], []
    for a in a_list:
        in_specs.append(pl.BlockSpec((tm, a.shape[1]), lambda i, j: (i, 0)))
        operands.append(a)
    for w, k, rb, cb in w_list:
        if w_transposed:
            in_specs.append(pl.BlockSpec((tn, k), lambda i, j, rb=rb, cb=cb: (j + cb, rb)))
        else:
            in_specs.append(pl.BlockSpec((k, tn), lambda i, j, rb=rb, cb=cb: (rb, j + cb)))
        operands.append(w)
    for arr, blk, imap in extras:
        in_specs.append(pl.BlockSpec(blk, imap))
        operands.append(arr)
    out_specs = [pl.BlockSpec((tm, tn), lambda i, j: (i, j)) for _ in out_dtypes]
    out_shape = [jax.ShapeDtypeStruct((m, n), dt) for dt in out_dtypes]
    return pl.pallas_call(
        functools.partial(_mm_kernel, n_a=len(a_list), n_e=len(extras), epilogue=epilogue,
                          w_transposed=w_transposed),
        grid=(m // tm, n // tn),
        in_specs=in_specs,
        out_specs=out_specs,
        out_shape=out_shape,
        compiler_params=_params(2),
        name=name,
    )(*operands)


def _ep_plain(acc, e_refs, o_refs):
    for o_ref in o_refs:
        o_ref[...] = acc.astype(o_ref.dtype)


def _ep_headnorm(acc, e_refs, o_refs, *, scale):
    g = e_refs[0][...]
    for c in range(acc.shape[1] // HEAD_DIM):
        blk = acc[:, c * HEAD_DIM:(c + 1) * HEAD_DIM]
        y = _rms(blk, g)
        if scale != 1.0:
            y = y * scale
        for o_ref in o_refs:
            o_ref[:, c * HEAD_DIM:(c + 1) * HEAD_DIM] = y.astype(o_ref.dtype)


def _ep_residual(acc, e_refs, o_refs):
    o_refs[0][...] = e_refs[0][...] + acc


def _ep_relu2(acc, e_refs, o_refs):
    r = jnp.maximum(acc, 0.0)
    o_refs[0][...] = (r * r).astype(o_refs[0].dtype)


def _ep_ple(acc, e_refs, o_refs):
    p_ref, wple_ref, h_ref = e_refs
    emb = _dot(p_ref[...].astype(BF16), wple_ref[...])
    o_refs[0][...] = h_ref[...] + emb * _sigmoid(acc)


def _softplus(z):
    return jnp.maximum(z, 0.0) + jnp.log(1.0 + jnp.exp(-jnp.abs(z)))


def _split_trunc(x):
    hi_f = lax.bitcast_convert_type(lax.bitcast_convert_type(x, jnp.uint32) & jnp.uint32(0xFFFF0000), F32)
    return hi_f.astype(BF16), (x - hi_f).astype(BF16)


def _softplus2(zz):
    neg_abs = lax.bitcast_convert_type(lax.bitcast_convert_type(zz, jnp.uint32) | jnp.uint32(0x80000000), F32)
    return jnp.maximum(zz, 0.0) + jnp.log(1.0 + jnp.exp2(neg_abs)) * LOG2E


def _sb_prompt_kernel(b_ref, q_ref, k_ref, v_ref, p_ref, o_ref, c_ref, acc_ref, *, tq):
    h = pl.program_id(0)
    i = pl.program_id(1)
    bias = b_ref[h]
    q = q_ref[...]
    p2 = p_ref[...]

    def tile(first_key, width, masked):
        start = pl.multiple_of(first_key, tq)
        kt = k_ref[pl.ds(start, width), :]
        vt = v_ref[pl.ds(start, width), :]
        zz = lax.dot_general(q, kt, NT_DIMS, preferred_element_type=F32) + bias
        sp = _softplus2(zz)
        if masked:
            mask = (lax.broadcasted_iota(jnp.int32, (tq, width), 1)
                    < lax.broadcasted_iota(jnp.int32, (tq, width), 0))
            sp = jnp.where(mask, sp, 0.0)
        hi, lo = _split_trunc(sp)
        c = c_ref[...]
        n_sub = width // LANES
        es = [None] * n_sub
        for u in reversed(range(n_sub)):
            sl = slice(u * LANES, (u + 1) * LANES)
            r = _dot(jnp.concatenate([hi[:, sl], lo[:, sl]], axis=1), p2)
            es[u] = jnp.exp2(zz[:, sl] + r[:, :LANES] + c)
            c = c + r[:, LANES:]
        c_ref[...] = c
        a = jnp.concatenate(es, axis=1)
        if masked:
            a = jnp.where(mask, a, 0.0)
        acc_ref[...] += _dot(a.astype(BF16), vt)

    c_ref[...] = jnp.zeros_like(c_ref)
    acc_ref[...] = jnp.zeros_like(acc_ref)
    tile(i * tq, tq, True)
    odd = i % 2

    @pl.when(odd == 1)
    def _():
        tile((i - 1) * tq, tq, False)

    def body(s, carry):
        tile((i - odd - 2 - 2 * s) * tq, 2 * tq, False)
        return carry

    lax.fori_loop(0, i // 2, body, 0)
    o_ref[...] = acc_ref[...].astype(o_ref.dtype)


def _cumsum_matrix(group, inclusive):
    lp = lax.broadcasted_iota(jnp.int32, (2 * LANES, 2 * LANES), 0) % LANES
    l = lax.broadcasted_iota(jnp.int32, (2 * LANES, 2 * LANES), 1)
    same = (lp % group) == (l % group)
    later = ((lp >= l) if inclusive else (lp > l)) | (l >= LANES)
    return jnp.where(same & later, -1.0, 0.0).astype(BF16)


def _sb_prompt(q, k, v, b_sb, tq):
    t = q.shape[0]
    return pl.pallas_call(
        functools.partial(_sb_prompt_kernel, tq=tq),
        grid=(N_HEADS, t // tq),
        in_specs=[pl.BlockSpec(memory_space=pltpu.SMEM),
                  pl.BlockSpec((tq, HEAD_DIM), lambda h, i: (i, h)),
                  pl.BlockSpec((t, HEAD_DIM), lambda h, i: (0, h)),
                  pl.BlockSpec((t, HEAD_DIM), lambda h, i: (0, h)),
                  pl.BlockSpec((2 * LANES, 2 * LANES), lambda h, i: (0, 0))],
        out_specs=pl.BlockSpec((tq, HEAD_DIM), lambda h, i: (i, h)),
        out_shape=jax.ShapeDtypeStruct((t, WIDTH), BF16),
        scratch_shapes=[pltpu.VMEM((tq, LANES), F32), pltpu.VMEM((tq, HEAD_DIM), F32)],
        compiler_params=_params(2),
        name="sb_prompt",
    )(b_sb, q, k, v, _cumsum_matrix(1, True))


def _sb_sample_kernel(pt_ref, q_ref, b_ref, p_ref, u_ref, *refs, n_pages):
    del pt_ref
    k_refs = refs[:n_pages]
    v_refs = refs[n_pages:2 * n_pages]
    o_ref, z_ref = refs[2 * n_pages:]
    rows = k_refs[0].shape[2] * N_HEADS
    n_tiles = rows // LANES

    q8 = q_ref[0] * QK_SCALE
    zero = jnp.zeros_like(q8)
    q16 = jnp.concatenate([jnp.concatenate([q8, zero], axis=1),
                           jnp.concatenate([zero, q8], axis=1)], axis=0).astype(BF16)
    diag = (lax.broadcasted_iota(jnp.int32, (N_HEADS, rows), 0)
            == lax.broadcasted_iota(jnp.int32, (N_HEADS, rows), 1) % N_HEADS)

    def pick(x):
        return jnp.sum(jnp.where(diag, x, 0.0), axis=0, keepdims=True)

    for pp in range(n_pages // 2):
        ka = k_refs[2 * pp][0, 0].reshape(rows, HEAD_DIM).astype(BF16)
        kb = k_refs[2 * pp + 1][0, 0].reshape(rows, HEAD_DIM).astype(BF16)
        res = lax.dot_general(q16, jnp.concatenate([ka, kb], axis=1), NT_DIMS, preferred_element_type=F32)
        z_ref[2 * pp:2 * pp + 1, :] = pick(res[:N_HEADS])
        z_ref[2 * pp + 1:2 * pp + 2, :] = pick(res[N_HEADS:])

    z = z_ref[...] + b_ref[...]
    sp = _softplus(z)
    hi, lo = _split_trunc(sp)
    p2 = p_ref[...]
    within, carry = [None] * n_tiles, [None] * n_tiles
    run = jnp.zeros((n_pages, LANES), F32)
    for c in reversed(range(n_tiles)):
        sl = slice(c * LANES, (c + 1) * LANES)
        r = _dot(jnp.concatenate([hi[:, sl], lo[:, sl]], axis=1), p2)
        within[c] = r[:, :LANES]
        carry[c] = run
        run = run + r[:, LANES:]
    later_pages = sum(_dot(u_ref[...], x) for x in _split3(run))
    t = z - sp
    a = jnp.concatenate([jnp.exp(t[:, c * LANES:(c + 1) * LANES] + within[c] + carry[c] + later_pages)
                         for c in range(n_tiles)], axis=1)

    acc = jnp.zeros((N_HEADS, HEAD_DIM), F32)
    for pp in range(n_pages // 2):
        sel = jnp.concatenate(
            [jnp.where(diag, jnp.broadcast_to(a[2 * pp:2 * pp + 1], (N_HEADS, rows)), 0.0),
             jnp.where(diag, jnp.broadcast_to(a[2 * pp + 1:2 * pp + 2], (N_HEADS, rows)), 0.0)],
            axis=0).astype(BF16)
        va = v_refs[2 * pp][0, 0].reshape(rows, HEAD_DIM).astype(BF16)
        vb = v_refs[2 * pp + 1][0, 0].reshape(rows, HEAD_DIM).astype(BF16)
        r = _dot(sel, jnp.concatenate([va, vb], axis=1))
        acc = acc + r[:N_HEADS, :HEAD_DIM] + r[N_HEADS:, HEAD_DIM:]
    o_ref[0] = acc.astype(o_ref.dtype)


def _sb_sample(q, b_sb, cache_k, cache_v, page_table):
    nb, n_pages = page_table.shape
    page = cache_k.shape[2]
    rows = page * N_HEADS
    b_row = jnp.tile(b_sb, page).reshape(1, rows)
    pi = lax.broadcasted_iota(jnp.int32, (n_pages, n_pages), 0)
    pj = lax.broadcasted_iota(jnp.int32, (n_pages, n_pages), 1)
    later = (pj > pi).astype(BF16)
    kv_specs = [pl.BlockSpec((1, 1, page, N_HEADS, HEAD_DIM), lambda b, pt, j=j: (0, pt[b, j], 0, 0, 0))
                for j in range(n_pages)]
    grid_spec = pltpu.PrefetchScalarGridSpec(
        num_scalar_prefetch=1,
        grid=(nb,),
        in_specs=[pl.BlockSpec((1, N_HEADS, HEAD_DIM), lambda b, pt: (b, 0, 0)),
                  pl.BlockSpec((1, rows), lambda b, pt: (0, 0)),
                  pl.BlockSpec((2 * LANES, 2 * LANES), lambda b, pt: (0, 0)),
                  pl.BlockSpec((n_pages, n_pages), lambda b, pt: (0, 0))] + kv_specs + kv_specs,
        out_specs=pl.BlockSpec((1, N_HEADS, HEAD_DIM), lambda b, pt: (b, 0, 0)),
        scratch_shapes=[pltpu.VMEM((n_pages, rows), F32)])
    return pl.pallas_call(
        functools.partial(_sb_sample_kernel, n_pages=n_pages),
        grid_spec=grid_spec,
        out_shape=jax.ShapeDtypeStruct((nb, N_HEADS, HEAD_DIM), BF16),
        compiler_params=_params(1),
        name="sb_sample",
    )(page_table, q, b_row, _cumsum_matrix(N_HEADS, False), later, *([cache_k] * n_pages), *([cache_v] * n_pages))


def _ml_prompt_kernel(q_ref, k_ref, v_ref, og_ref, g_ref, gt_ref, gmh_ref, tril_ref, triu_ref,
                      hm_ref, c_ref, n_ref, m_ref, *, chunk):
    step = pl.program_id(0)

    @pl.when(step == 0)
    def _():
        c_ref[...] = jnp.zeros_like(c_ref)
        n_ref[...] = jnp.zeros_like(n_ref)
        m_ref[...] = jnp.zeros_like(m_ref)

    g = g_ref[...]
    gt = gt_ref[...]
    lane = lax.broadcasted_iota(jnp.int32, g.shape, 1)
    rowi = lax.broadcasted_iota(jnp.int32, gt.shape, 0)
    gl = jnp.where(lane >= N_HEADS, _log_sigmoid(g), g)
    gtl = jnp.where(rowi >= N_HEADS, _log_sigmoid(gt), gt)
    tril = tril_ref[...]
    triu = triu_ref[...]
    a_col = sum(_dot(tril, x) for x in _split3(gl))
    a_row = sum(_dot(x, triu) for x in _split3(gtl))
    tt = lax.broadcasted_iota(jnp.int32, (chunk, chunk), 0)
    ss = lax.broadcasted_iota(jnp.int32, (chunk, chunk), 1)
    causal = ss <= tt
    gmh = gmh_ref[...]

    for h in range(N_HEADS):
        sl = slice(h * HEAD_DIM, (h + 1) * HEAD_DIM)
        b_col = a_col[:, N_HEADS + h:N_HEADS + h + 1]
        li_col = gl[:, h:h + 1]
        b_row = a_row[N_HEADS + h:N_HEADS + h + 1, :]
        li_row = gtl[h:h + 1, :]
        m_old = m_ref[h:h + 1, 0:1]
        b_end = b_col[chunk - 1:chunk, :]

        dmat = b_col - b_row + li_row
        inter = b_col + m_old
        m_row = jnp.maximum(inter, jnp.max(jnp.where(causal, dmat, -jnp.inf), axis=-1, keepdims=True))
        e = jnp.where(causal, jnp.exp(dmat - m_row), 0.0)
        w_inter = jnp.exp(inter - m_row)

        qc = q_ref[:, sl]
        kc = k_ref[:, sl] * QK_SCALE
        vc = v_ref[:, sl]
        qb = qc.astype(BF16)
        kb = kc.astype(BF16)
        s = lax.dot_general(qb, kb, NT_DIMS, preferred_element_type=F32) * e
        c_old = c_ref[h]
        n_old = n_ref[h:h + 1, :]
        cq = lax.dot_general(qb, c_old.astype(BF16), NT_DIMS, preferred_element_type=F32)
        num = _dot(s.astype(BF16), vc.astype(BF16)) + w_inter * cq
        den = jnp.sum(s, axis=-1, keepdims=True) + w_inter * jnp.sum(qc * n_old, axis=-1, keepdims=True)
        hh = num / jnp.maximum(jnp.abs(den), jnp.exp(-m_row))
        hm_ref[:, sl] = (_rms(hh, gmh) * _sigmoid(og_ref[:, sl])).astype(hm_ref.dtype)

        gw = b_end - b_col + li_col
        m_new = jnp.maximum(b_end + m_old, jnp.max(gw, axis=0, keepdims=True))
        wk = jnp.exp(gw - m_new)
        decay = jnp.exp(b_end + m_old - m_new)
        c_ref[h] = decay * c_old + lax.dot_general((wk * vc).astype(BF16), kb, TN_DIMS,
                                                   preferred_element_type=F32)
        n_ref[h:h + 1, :] = decay * n_old + jnp.sum(wk * kc, axis=0, keepdims=True)
        m_ref[h:h + 1, :] = jnp.broadcast_to(m_new, (1, LANES))


def _ml_prompt(ml, gates, gates_t, g_mh, chunk):
    t = ml.shape[0]
    tri = jnp.tril(jnp.ones((chunk, chunk), F32)).astype(BF16)
    col = lambda c: pl.BlockSpec((chunk, WIDTH), lambda s, c=c: (s, c))
    return pl.pallas_call(
        functools.partial(_ml_prompt_kernel, chunk=chunk),
        grid=(t // chunk,),
        in_specs=[col(0), col(1), col(2), col(3),
                  pl.BlockSpec((chunk, LANES), lambda s: (s, 0)),
                  pl.BlockSpec((2 * N_HEADS, chunk), lambda s: (0, s)),
                  pl.BlockSpec((1, HEAD_DIM), lambda s: (0, 0)),
                  pl.BlockSpec((chunk, chunk), lambda s: (0, 0)),
                  pl.BlockSpec((chunk, chunk), lambda s: (0, 0))],
        out_specs=[pl.BlockSpec((chunk, WIDTH), lambda s: (s, 0)),
                   pl.BlockSpec((N_HEADS, HEAD_DIM, HEAD_DIM), lambda s: (0, 0, 0)),
                   pl.BlockSpec((N_HEADS, HEAD_DIM), lambda s: (0, 0)),
                   pl.BlockSpec((N_HEADS, LANES), lambda s: (0, 0))],
        out_shape=[jax.ShapeDtypeStruct((t, WIDTH), BF16),
                   jax.ShapeDtypeStruct((N_HEADS, HEAD_DIM, HEAD_DIM), F32),
                   jax.ShapeDtypeStruct((N_HEADS, HEAD_DIM), F32),
                   jax.ShapeDtypeStruct((N_HEADS, LANES), F32)],
        compiler_params=_params(1),
        name="mlstm_prompt",
    )(ml, ml, ml, ml, gates, gates_t, g_mh.reshape(1, HEAD_DIM), tri, tri.T)


def _ml_sample_kernel(x_ref, g_ref, m_ref, c_ref, n_ref, gmh_ref, hm_ref, cn_ref, nn_ref, mn_ref):
    for b in range(x_ref.shape[0]):
        _ml_sample_one(b, x_ref, g_ref, m_ref, c_ref, n_ref, gmh_ref, hm_ref, cn_ref, nn_ref, mn_ref)


def _ml_sample_one(b, x_ref, g_ref, m_ref, c_ref, n_ref, gmh_ref, hm_ref, cn_ref, nn_ref, mn_ref):
    x = x_ref[b]
    q = x[0:N_HEADS]
    k = x[N_HEADS:2 * N_HEADS] * QK_SCALE
    v = x[2 * N_HEADS:3 * N_HEADS]
    og = x[3 * N_HEADS:4 * N_HEADS]
    gates = g_ref[b]
    li = gates[0:N_HEADS]
    lf = _log_sigmoid(gates[N_HEADS:2 * N_HEADS])
    m_old = m_ref[b]
    n_old = n_ref[b]

    inter = lf + m_old
    m_new = jnp.maximum(inter, li)
    s = jnp.sum(q * k, axis=-1, keepdims=True) * jnp.exp(li - m_new)
    w_inter = jnp.exp(inter - m_new)
    qb = q.astype(BF16)
    rowh = lax.broadcasted_iota(jnp.int32, (N_HEADS, HEAD_DIM), 0)
    cq = jnp.zeros((N_HEADS, HEAD_DIM), F32)
    for h in range(N_HEADS):
        r = lax.dot_general(qb, c_ref[b, h].astype(BF16), NT_DIMS, preferred_element_type=F32)
        cq = jnp.where(rowh == h, r, cq)
    num = s * v + w_inter * cq
    den = s + w_inter * jnp.sum(n_old * q, axis=-1, keepdims=True)
    hh = num / jnp.maximum(jnp.abs(den), jnp.exp(-m_new))
    hm_ref[b] = (_rms(hh, gmh_ref[...]) * _sigmoid(og)).astype(hm_ref.dtype)

    wk = jnp.exp(li - m_new)
    decay = w_inter
    nn_ref[b] = decay * n_old + wk * k
    mn_ref[b] = m_new
    wv = wk * v
    wvt = jnp.concatenate([wv, jnp.zeros((LANES - N_HEADS, HEAD_DIM), F32)], axis=0).T
    for h in range(N_HEADS):
        cn_ref[b, h] = decay[h:h + 1, :] * c_ref[b, h] + wvt[:, h:h + 1] * k[h:h + 1, :]


def _ml_sample(ml, gates, state_m, state_c, state_n, g_mh, group):
    nb = ml.shape[0]
    x = ml.reshape(nb, 4 * N_HEADS, HEAD_DIM)
    g = gates.reshape(nb, 2 * N_HEADS, 1)
    m = state_m.reshape(nb, N_HEADS, 1)
    b3 = lambda *shape: pl.BlockSpec((group,) + shape, lambda b: (b,) + (0,) * len(shape))
    return pl.pallas_call(
        _ml_sample_kernel,
        grid=(nb // group,),
        in_specs=[b3(4 * N_HEADS, HEAD_DIM), b3(2 * N_HEADS, 1), b3(N_HEADS, 1),
                  b3(N_HEADS, HEAD_DIM, HEAD_DIM), b3(N_HEADS, HEAD_DIM),
                  pl.BlockSpec((1, HEAD_DIM), lambda b: (0, 0))],
        out_specs=[b3(N_HEADS, HEAD_DIM), b3(N_HEADS, HEAD_DIM, HEAD_DIM), b3(N_HEADS, HEAD_DIM), b3(N_HEADS, 1)],
        out_shape=[jax.ShapeDtypeStruct((nb, N_HEADS, HEAD_DIM), BF16),
                   jax.ShapeDtypeStruct((nb, N_HEADS, HEAD_DIM, HEAD_DIM), F32),
                   jax.ShapeDtypeStruct((nb, N_HEADS, HEAD_DIM), F32),
                   jax.ShapeDtypeStruct((nb, N_HEADS, 1), F32)],
        compiler_params=_params(1),
        name="mlstm_sample",
    )(x, g, m, state_c, state_n, g_mh.reshape(1, HEAD_DIM))


def _dense_front(x, w, tm, q_scale, q_dtype, kv_dtypes):
    xn, gates = _norm_gate(x, w["g_mix"], w["wg_hi"], w["wg_lo"], w["bg"], min(tm, 512))
    d = x.shape[1]
    gq = (w["g_q"].reshape(1, HEAD_DIM), (1, HEAD_DIM), lambda i, j: (0, 0))
    gk = (w["g_k"].reshape(1, HEAD_DIM), (1, HEAD_DIM), lambda i, j: (0, 0))
    tn = 1024
    w_in_t = w["w_in_t"]
    (q,) = _matmul("proj_q", [xn], [(w_in_t, d, 0, 0)], [gq],
                   functools.partial(_ep_headnorm, scale=q_scale), [q_dtype], WIDTH, tm, tn, True)
    k = _matmul("proj_k", [xn], [(w_in_t, d, 0, WIDTH // tn)], [gk],
                functools.partial(_ep_headnorm, scale=1.0), kv_dtypes, WIDTH, tm, tn, True)
    v = _matmul("proj_v", [xn], [(w_in_t, d, 0, 2 * WIDTH // tn)], [], _ep_plain, kv_dtypes, WIDTH, tm, tn, True)
    (ml,) = _matmul("proj_ml", [xn], [(w_in_t, d, 0, 3 * WIDTH // tn)], [], _ep_plain, [F32], 4 * WIDTH, tm, tn,
                    True)
    return q, k, v, ml, gates


def _dense_back(x, sb, hm, p, w, tm):
    m, d = x.shape
    tn = 1024
    res = lambda arr: (arr, (tm, tn), lambda i, j: (i, j))
    (h1,) = _matmul("out_proj", [sb, hm], [(w["w_out"], WIDTH, 0, 0), (w["w_out"], WIDTH, 1, 0)], [res(x)],
                    _ep_residual, [F32], d, tm, tn)
    hn = _norm(h1, w["g_ffn"], min(tm, 512))
    d_ff = w["w_up"].shape[1]
    (act,) = _matmul("ffn_up", [hn], [(w["w_up"], d, 0, 0)], [], _ep_relu2, [BF16], d_ff, tm, tn)
    tm2, tn2 = min(tm, 512), 512
    res2 = lambda arr: (arr, (tm2, tn2), lambda i, j: (i, j))
    (h2,) = _matmul("ffn_down", [act], [(w["w_down"], d_ff, 0, 0)], [res2(h1)], _ep_residual, [F32], d, tm2, tn2)
    hn2 = _norm(h2, w["g_ple"], min(tm, 512))
    ple = p.shape[1]
    extras = [(p, (tm, ple), lambda i, j: (i, 0)),
              (w["w_ple"], (ple, tn), lambda i, j: (0, j)),
              res(h2)]
    (y,) = _matmul("ple_gate", [hn2], [(w["w_pg"], d, 0, 0)], extras, _ep_ple, [F32], d, tm, tn)
    return y


def kernel(x_prompt, x_sample, cache_k, cache_v, state_C, state_n, state_m, page_table, p_prompt, p_sample,
           g_mix, w_in, b_gate, b_sb, g_q, g_k, g_mh, w_out, g_ffn, w_up, w_down, g_ple, w_ple, w_pg):
    depth = w_in.shape[0]
    assert depth == 1 and x_prompt.shape[0] == 1 and x_sample.shape[1] == 1
    t, d = x_prompt.shape[1], x_prompt.shape[2]
    nb = x_sample.shape[0]
    n_proj = w_in.shape[2] - 2 * N_HEADS

    wg = jnp.pad(w_in[0, :, n_proj:], ((0, 0), (0, LANES - 2 * N_HEADS)))
    wg_hi = wg.astype(BF16)
    w = {
        "g_mix": g_mix[0], "g_q": g_q[0], "g_k": g_k[0], "g_ffn": g_ffn[0], "g_ple": g_ple[0],
        "wg_hi": wg_hi, "wg_lo": (wg - wg_hi.astype(F32)).astype(BF16),
        "bg": jnp.pad(b_gate[0], (0, LANES - 2 * N_HEADS)).reshape(1, LANES),
        "w_in_t": jnp.swapaxes(w_in[0], 0, 1).astype(BF16), "w_out": w_out[0].astype(BF16), "w_up": w_up[0].astype(BF16),
        "w_down": w_down[0].astype(BF16), "w_ple": w_ple[0].astype(BF16), "w_pg": w_pg[0].astype(BF16),
    }

    xp = x_prompt[0]
    q, (k, k_bf), (v, v_bf), ml, gates = _dense_front(xp, w, 1024, QK_SCALE * LOG2E, BF16, [F32, BF16])
    sb = _sb_prompt(q, k_bf, v_bf, b_sb[0] * LOG2E, 512)
    chunk = 256
    gates_t = gates[:, :2 * N_HEADS].T
    hm, c_p, n_p, m_p = _ml_prompt(ml, gates, gates_t, g_mh[0], chunk)
    y_p = _dense_back(xp, sb, hm, p_prompt[0, 0], w, 1024)

    xs = x_sample[:, 0]
    qs, (ks,), (vs,), mls, gates_s = _dense_front(xs, w, nb, 1.0, F32, [F32])
    sbs = _sb_sample(qs.reshape(nb, N_HEADS, HEAD_DIM), b_sb[0], cache_k, cache_v, page_table)
    hms, c_s, n_s, m_s = _ml_sample(mls, gates_s[:, :2 * N_HEADS], state_m[0], state_C[0], state_n[0], g_mh[0], 8)
    y_s = _dense_back(xs, sbs.reshape(nb, WIDTH), hms.reshape(nb, WIDTH), p_sample[0, :, 0], w, nb)

    hd = (N_HEADS, HEAD_DIM)
    return (y_p[None], y_s[:, None],
            k.reshape((1, 1, t) + hd), v.reshape((1, 1, t) + hd),
            c_p[None, None], n_p[None, None], m_p[:, 0][None, None],
            ks.reshape((1, nb, 1) + hd), vs.reshape((1, nb, 1) + hd),
            c_s[None], n_s[None], m_s.reshape(1, nb, N_HEADS))
```

```python
import functools
import math

import jax
import jax.numpy as jnp
from jax import lax
from jax.experimental import pallas as pl
from jax.experimental.pallas import tpu as pltpu

F32 = jnp.float32
BF16 = jnp.bfloat16

HEAD_DIM = 128
N_HEADS = 8
WIDTH = N_HEADS * HEAD_DIM
RMS_EPS = 1e-6
QK_SCALE = HEAD_DIM ** -0.5
LOG2E = 1.0 / math.log(2.0)
LANES = 128
VMEM_LIMIT = 56 * 1024 * 1024

NT_DIMS = (((1,), (1,)), ((), ()))
TN_DIMS = (((0,), (0,)), ((), ()))


def _params(n_grid_dims):
    return pltpu.CompilerParams(
        dimension_semantics=("arbitrary",) * n_grid_dims,
        vmem_limit_bytes=VMEM_LIMIT)


def _log_sigmoid(x):
    return jnp.minimum(x, 0.0) - jnp.log1p(jnp.exp(-jnp.abs(x)))


def _sigmoid(x):
    return 1.0 / (1.0 + jnp.exp(-x))


def _split2(x):
    hi = x.astype(BF16)
    lo = (x - hi.astype(F32)).astype(BF16)
    return hi, lo


def _split3(x):
    x1 = x.astype(BF16)
    r1 = x - x1.astype(F32)
    x2 = r1.astype(BF16)
    x3 = (r1 - x2.astype(F32)).astype(BF16)
    return x1, x2, x3


def _dot(a, b):
    return jnp.dot(a, b, preferred_element_type=F32)


def _rms(x, g):
    return x * lax.rsqrt(jnp.mean(x * x, axis=-1, keepdims=True) + RMS_EPS) * g


def _norm_kernel(x_ref, g_ref, o_ref):
    o_ref[...] = _rms(x_ref[...], g_ref[...]).astype(o_ref.dtype)


def _norm_gate_kernel(x_ref, g_ref, wh_ref, wl_ref, b_ref, o_ref, gate_ref):
    y = _rms(x_ref[...], g_ref[...])
    yh, yl = _split2(y)
    o_ref[...] = yh
    wh = wh_ref[...]
    gate_ref[...] = _dot(yh, wh) + _dot(yh, wl_ref[...]) + _dot(yl, wh) + b_ref[...]


def _norm(x, g, tm):
    m, d = x.shape
    return pl.pallas_call(
        _norm_kernel,
        grid=(m // tm,),
        in_specs=[pl.BlockSpec((tm, d), lambda i: (i, 0)),
                  pl.BlockSpec((1, d), lambda i: (0, 0))],
        out_specs=pl.BlockSpec((tm, d), lambda i: (i, 0)),
        out_shape=jax.ShapeDtypeStruct((m, d), BF16),
        compiler_params=_params(1),
        name="rmsnorm",
    )(x, g.reshape(1, d))


def _norm_gate(x, g, wg_hi, wg_lo, bg, tm):
    m, d = x.shape
    return pl.pallas_call(
        _norm_gate_kernel,
        grid=(m // tm,),
        in_specs=[pl.BlockSpec((tm, d), lambda i: (i, 0)),
                  pl.BlockSpec((1, d), lambda i: (0, 0)),
                  pl.BlockSpec((d, LANES), lambda i: (0, 0)),
                  pl.BlockSpec((d, LANES), lambda i: (0, 0)),
                  pl.BlockSpec((1, LANES), lambda i: (0, 0))],
        out_specs=[pl.BlockSpec((tm, d), lambda i: (i, 0)),
                   pl.BlockSpec((tm, LANES), lambda i: (i, 0))],
        out_shape=[jax.ShapeDtypeStruct((m, d), BF16),
                   jax.ShapeDtypeStruct((m, LANES), F32)],
        compiler_params=_params(1),
        name="rmsnorm_gates",
    )(x, g.reshape(1, d), wg_hi, wg_lo, bg)


def _mm_kernel(*refs, n_a, n_e, epilogue, w_transposed):
    a_refs = refs[:n_a]
    w_refs = refs[n_a:2 * n_a]
    e_refs = refs[2 * n_a:2 * n_a + n_e]
    o_refs = refs[2 * n_a + n_e:]
    acc = None
    for a_ref, w_ref in zip(a_refs, w_refs):
        if w_transposed:
            d = lax.dot_general(a_ref[...], w_ref[...], NT_DIMS, preferred_element_type=F32)
        else:
            d = _dot(a_ref[...], w_ref[...])
        acc = d if acc is None else acc + d
    epilogue(acc, e_refs, o_refs)


def _matmul(name, a_list, w_list, extras, epilogue, out_dtypes, n, tm, tn, w_transposed=False):
    m = a_list[0].shape[0]
    in_specs, operands = [], []
    for a in a_list:
        in_specs.append(pl.BlockSpec((tm, a.shape[1]), lambda i, j: (i, 0)))
        operands.append(a)
    for w, k, rb, cb in w_list:
        if w_transposed:
            in_specs.append(pl.BlockSpec((tn, k), lambda i, j, rb=rb, cb=cb: (j + cb, rb)))
        else:
            in_specs.append(pl.BlockSpec((k, tn), lambda i, j, rb=rb, cb=cb: (rb, j + cb)))
        operands.append(w)
    for arr, blk, imap in extras:
        in_specs.append(pl.BlockSpec(blk, imap))
        operands.append(arr)
    out_specs = [pl.BlockSpec((tm, tn), lambda i, j: (i, j)) for _ in out_dtypes]
    out_shape = [jax.ShapeDtypeStruct((m, n), dt) for dt in out_dtypes]
    return pl.pallas_call(
        functools.partial(_mm_kernel, n_a=len(a_list), n_e=len(extras), epilogue=epilogue,
                          w_transposed=w_transposed),
        grid=(m // tm, n // tn),
        in_specs=in_specs,
        out_specs=out_specs,
        out_shape=out_shape,
        compiler_params=_params(2),
        name=name,
    )(*operands)


def _ep_plain(acc, e_refs, o_refs):
    for o_ref in o_refs:
        o_ref[...] = acc.astype(o_ref.dtype)


def _ep_headnorm(acc, e_refs, o_refs, *, scale):
    g = e_refs[0][...]
    for c in range(acc.shape[1] // HEAD_DIM):
        blk = acc[:, c * HEAD_DIM:(c + 1) * HEAD_DIM]
        y = _rms(blk, g)
        if scale != 1.0:
            y = y * scale
        for o_ref in o_refs:
            o_ref[:, c * HEAD_DIM:(c + 1) * HEAD_DIM] = y.astype(o_ref.dtype)


def _ep_residual(acc, e_refs, o_refs):
    o_refs[0][...] = e_refs[0][...] + acc


def _ep_relu2(acc, e_refs, o_refs):
    r = jnp.maximum(acc, 0.0)
    o_refs[0][...] = (r * r).astype(o_refs[0].dtype)


def _ep_ple(acc, e_refs, o_refs):
    p_ref, wple_ref, h_ref = e_refs
    emb = _dot(p_ref[...].astype(BF16), wple_ref[...])
    o_refs[0][...] = h_ref[...] + emb * _sigmoid(acc)


def _softplus(z):
    return jnp.maximum(z, 0.0) + jnp.log(1.0 + jnp.exp(-jnp.abs(z)))


def _split_trunc(x):
    hi_f = lax.bitcast_convert_type(lax.bitcast_convert_type(x, jnp.uint32) & jnp.uint32(0xFFFF0000), F32)
    return hi_f.astype(BF16), (x - hi_f).astype(BF16)


def _softplus2(zz):
    neg_abs = lax.bitcast_convert_type(lax.bitcast_convert_type(zz, jnp.uint32) | jnp.uint32(0x80000000), F32)
    return jnp.maximum(zz, 0.0) + jnp.log(1.0 + jnp.exp2(neg_abs)) * LOG2E


def _sb_prompt_body(h, i, b_ref, q_ref, k_ref, v_ref, p_ref, o_ref, c_ref, acc_ref, tq):
    bias = b_ref[h]
    q = q_ref[...]
    p2 = p_ref[...]

    def tile(first_key, width, masked):
        start = pl.multiple_of(first_key, tq)
        kt = k_ref[pl.ds(start, width), :]
        vt = v_ref[pl.ds(start, width), :]
        zz = lax.dot_general(q, kt, NT_DIMS, preferred_element_type=F32) + bias
        sp = _softplus2(zz)
        if masked:
            mask = (lax.broadcasted_iota(jnp.int32, (tq, width), 1)
                    < lax.broadcasted_iota(jnp.int32, (tq, width), 0))
            sp = jnp.where(mask, sp, 0.0)
        hi, lo = _split_trunc(sp)
        c = c_ref[...]
        n_sub = width // LANES
        es = [None] * n_sub
        for u in reversed(range(n_sub)):
            sl = slice(u * LANES, (u + 1) * LANES)
            r = _dot(jnp.concatenate([hi[:, sl], lo[:, sl]], axis=1), p2)
            es[u] = jnp.exp2(zz[:, sl] + r[:, :LANES] + c)
            c = c + r[:, LANES:]
        c_ref[...] = c
        a = jnp.concatenate(es, axis=1)
        if masked:
            a = jnp.where(mask, a, 0.0)
        acc_ref[...] += _dot(a.astype(BF16), vt)

    c_ref[...] = jnp.zeros_like(c_ref)
    acc_ref[...] = jnp.zeros_like(acc_ref)
    tile(i * tq, tq, True)
    odd = i % 2

    @pl.when(odd == 1)
    def _():
        tile((i - 1) * tq, tq, False)

    def body(s, carry):
        tile((i - odd - 2 - 2 * s) * tq, 2 * tq, False)
        return carry

    lax.fori_loop(0, i // 2, body, 0)
    o_ref[...] = acc_ref[...].astype(o_ref.dtype)


def _cumsum_matrix(group, inclusive):
    lp = lax.broadcasted_iota(jnp.int32, (2 * LANES, 2 * LANES), 0) % LANES
    l = lax.broadcasted_iota(jnp.int32, (2 * LANES, 2 * LANES), 1)
    same = (lp % group) == (l % group)
    later = ((lp >= l) if inclusive else (lp > l)) | (l >= LANES)
    return jnp.where(same & later, -1.0, 0.0).astype(BF16)


def _sb_sample_body(q_ref, b_ref, p_ref, u_ref, k_refs, v_refs, o_ref, z_ref):
    n_pages = len(k_refs)
    rows = k_refs[0].shape[2] * N_HEADS
    n_tiles = rows // LANES

    q8 = q_ref[0] * QK_SCALE
    zero = jnp.zeros_like(q8)
    q16 = jnp.concatenate([jnp.concatenate([q8, zero], axis=1),
                           jnp.concatenate([zero, q8], axis=1)], axis=0).astype(BF16)
    diag = (lax.broadcasted_iota(jnp.int32, (N_HEADS, rows), 0)
            == lax.broadcasted_iota(jnp.int32, (N_HEADS, rows), 1) % N_HEADS)

    def pick(x):
        return jnp.sum(jnp.where(diag, x, 0.0), axis=0, keepdims=True)

    for pp in range(n_pages // 2):
        ka = k_refs[2 * pp][0, 0].reshape(rows, HEAD_DIM).astype(BF16)
        kb = k_refs[2 * pp + 1][0, 0].reshape(rows, HEAD_DIM).astype(BF16)
        res = lax.dot_general(q16, jnp.concatenate([ka, kb], axis=1), NT_DIMS, preferred_element_type=F32)
        z_ref[2 * pp:2 * pp + 1, :] = pick(res[:N_HEADS])
        z_ref[2 * pp + 1:2 * pp + 2, :] = pick(res[N_HEADS:])

    z = z_ref[...] + b_ref[...]
    sp = _softplus(z)
    hi, lo = _split_trunc(sp)
    p2 = p_ref[...]
    within, carry = [None] * n_tiles, [None] * n_tiles
    run = jnp.zeros((n_pages, LANES), F32)
    for c in reversed(range(n_tiles)):
        sl = slice(c * LANES, (c + 1) * LANES)
        r = _dot(jnp.concatenate([hi[:, sl], lo[:, sl]], axis=1), p2)
        within[c] = r[:, :LANES]
        carry[c] = run
        run = run + r[:, LANES:]
    later_pages = sum(_dot(u_ref[...], x) for x in _split3(run))
    t = z - sp
    a = jnp.concatenate([jnp.exp(t[:, c * LANES:(c + 1) * LANES] + within[c] + carry[c] + later_pages)
                         for c in range(n_tiles)], axis=1)

    acc = jnp.zeros((N_HEADS, HEAD_DIM), F32)
    for pp in range(n_pages // 2):
        sel = jnp.concatenate(
            [jnp.where(diag, jnp.broadcast_to(a[2 * pp:2 * pp + 1], (N_HEADS, rows)), 0.0),
             jnp.where(diag, jnp.broadcast_to(a[2 * pp + 1:2 * pp + 2], (N_HEADS, rows)), 0.0)],
            axis=0).astype(BF16)
        va = v_refs[2 * pp][0, 0].reshape(rows, HEAD_DIM).astype(BF16)
        vb = v_refs[2 * pp + 1][0, 0].reshape(rows, HEAD_DIM).astype(BF16)
        r = _dot(sel, jnp.concatenate([va, vb], axis=1))
        acc = acc + r[:N_HEADS, :HEAD_DIM] + r[N_HEADS:, HEAD_DIM:]
    o_ref[0] = acc.astype(o_ref.dtype)


def _sb_attention_kernel(pt_ref, bp_ref, qp_ref, kp_ref, vp_ref, pp_ref, qs_ref, bs_ref, ps_ref, u_ref, *refs,
                         n_pages, tq):
    del pt_ref
    k_refs = refs[:n_pages]
    v_refs = refs[n_pages:2 * n_pages]
    op_ref, os_ref, c_ref, acc_ref, z_ref = refs[2 * n_pages:]
    _sb_sample_body(qs_ref, bs_ref, ps_ref, u_ref, k_refs, v_refs, os_ref, z_ref)
    _sb_prompt_body(pl.program_id(0), pl.program_id(1), bp_ref, qp_ref, kp_ref, vp_ref, pp_ref, op_ref,
                    c_ref, acc_ref, tq)


def _sb_attention(q, k, v, b_sb, tq, qs, cache_k, cache_v, page_table):
    t = q.shape[0]
    n_blocks = t // tq
    nb, n_pages = page_table.shape
    assert nb == N_HEADS * n_blocks, "one sample sequence per (head, query block) grid step"
    page = cache_k.shape[2]
    rows = page * N_HEADS
    b_row = jnp.tile(b_sb, page).reshape(1, rows)
    pi = lax.broadcasted_iota(jnp.int32, (n_pages, n_pages), 0)
    pj = lax.broadcasted_iota(jnp.int32, (n_pages, n_pages), 1)
    later = (pj > pi).astype(BF16)
    seq = lambda h, i: h * n_blocks + i
    kv_specs = [pl.BlockSpec((1, 1, page, N_HEADS, HEAD_DIM),
                             lambda h, i, pt, j=j: (0, pt[seq(h, i), j], 0, 0, 0))
                for j in range(n_pages)]
    const = lambda shape: pl.BlockSpec(shape, lambda h, i, pt: (0,) * len(shape))
    grid_spec = pltpu.PrefetchScalarGridSpec(
        num_scalar_prefetch=1,
        grid=(N_HEADS, n_blocks),
        in_specs=[pl.BlockSpec(memory_space=pltpu.SMEM),
                  pl.BlockSpec((tq, HEAD_DIM), lambda h, i, pt: (i, h)),
                  pl.BlockSpec((t, HEAD_DIM), lambda h, i, pt: (0, h)),
                  pl.BlockSpec((t, HEAD_DIM), lambda h, i, pt: (0, h)),
                  const((2 * LANES, 2 * LANES)),
                  pl.BlockSpec((1, N_HEADS, HEAD_DIM), lambda h, i, pt: (seq(h, i), 0, 0)),
                  const((1, rows)),
                  const((2 * LANES, 2 * LANES)),
                  const((n_pages, n_pages))] + kv_specs + kv_specs,
        out_specs=[pl.BlockSpec((tq, HEAD_DIM), lambda h, i, pt: (i, h)),
                   pl.BlockSpec((1, N_HEADS, HEAD_DIM), lambda h, i, pt: (seq(h, i), 0, 0))],
        scratch_shapes=[pltpu.VMEM((tq, LANES), F32), pltpu.VMEM((tq, HEAD_DIM), F32),
                        pltpu.VMEM((n_pages, rows), F32)])
    return pl.pallas_call(
        functools.partial(_sb_attention_kernel, n_pages=n_pages, tq=tq),
        grid_spec=grid_spec,
        out_shape=[jax.ShapeDtypeStruct((t, WIDTH), BF16),
                   jax.ShapeDtypeStruct((nb, N_HEADS, HEAD_DIM), BF16)],
        compiler_params=_params(2),
        name="sb_attention",
    )(page_table, b_sb * LOG2E, q, k, v, _cumsum_matrix(1, True),
      qs, b_row, _cumsum_matrix(N_HEADS, False), later, *([cache_k] * n_pages), *([cache_v] * n_pages))


def _ml_prompt_kernel(q_ref, k_ref, v_ref, og_ref, g_ref, gt_ref, gmh_ref, tril_ref, triu_ref,
                      hm_ref, c_ref, n_ref, m_ref, *, chunk):
    step = pl.program_id(0)

    @pl.when(step == 0)
    def _():
        c_ref[...] = jnp.zeros_like(c_ref)
        n_ref[...] = jnp.zeros_like(n_ref)
        m_ref[...] = jnp.zeros_like(m_ref)

    g = g_ref[...]
    gt = gt_ref[...]
    lane = lax.broadcasted_iota(jnp.int32, g.shape, 1)
    rowi = lax.broadcasted_iota(jnp.int32, gt.shape, 0)
    gl = jnp.where(lane >= N_HEADS, _log_sigmoid(g), g)
    gtl = jnp.where(rowi >= N_HEADS, _log_sigmoid(gt), gt)
    tril = tril_ref[...]
    triu = triu_ref[...]
    a_col = sum(_dot(tril, x) for x in _split3(gl))
    a_row = sum(_dot(x, triu) for x in _split3(gtl))
    tt = lax.broadcasted_iota(jnp.int32, (chunk, chunk), 0)
    ss = lax.broadcasted_iota(jnp.int32, (chunk, chunk), 1)
    causal = ss <= tt
    gmh = gmh_ref[...]

    for h in range(N_HEADS):
        sl = slice(h * HEAD_DIM, (h + 1) * HEAD_DIM)
        b_col = a_col[:, N_HEADS + h:N_HEADS + h + 1]
        li_col = gl[:, h:h + 1]
        b_row = a_row[N_HEADS + h:N_HEADS + h + 1, :]
        li_row = gtl[h:h + 1, :]
        m_old = m_ref[h:h + 1, 0:1]
        b_end = b_col[chunk - 1:chunk, :]

        dmat = b_col - b_row + li_row
        inter = b_col + m_old
        m_row = jnp.maximum(inter, jnp.max(jnp.where(causal, dmat, -jnp.inf), axis=-1, keepdims=True))
        e = jnp.where(causal, jnp.exp(dmat - m_row), 0.0)
        w_inter = jnp.exp(inter - m_row)

        qc = q_ref[:, sl]
        kc = k_ref[:, sl] * QK_SCALE
        vc = v_ref[:, sl]
        qb = qc.astype(BF16)
        kb = kc.astype(BF16)
        s = lax.dot_general(qb, kb, NT_DIMS, preferred_element_type=F32) * e
        c_old = c_ref[h]
        n_old = n_ref[h:h + 1, :]
        cq = lax.dot_general(qb, c_old.astype(BF16), NT_DIMS, preferred_element_type=F32)
        num = _dot(s.astype(BF16), vc.astype(BF16)) + w_inter * cq
        den = jnp.sum(s, axis=-1, keepdims=True) + w_inter * jnp.sum(qc * n_old, axis=-1, keepdims=True)
        hh = num / jnp.maximum(jnp.abs(den), jnp.exp(-m_row))
        hm_ref[:, sl] = (_rms(hh, gmh) * _sigmoid(og_ref[:, sl])).astype(hm_ref.dtype)

        gw = b_end - b_col + li_col
        m_new = jnp.maximum(b_end + m_old, jnp.max(gw, axis=0, keepdims=True))
        wk = jnp.exp(gw - m_new)
        decay = jnp.exp(b_end + m_old - m_new)
        c_ref[h] = decay * c_old + lax.dot_general((wk * vc).astype(BF16), kb, TN_DIMS,
                                                   preferred_element_type=F32)
        n_ref[h:h + 1, :] = decay * n_old + jnp.sum(wk * kc, axis=0, keepdims=True)
        m_ref[h:h + 1, :] = jnp.broadcast_to(m_new, (1, LANES))


def _ml_prompt(ml, gates, gates_t, g_mh, chunk):
    t = ml.shape[0]
    tri = jnp.tril(jnp.ones((chunk, chunk), F32)).astype(BF16)
    col = lambda c: pl.BlockSpec((chunk, WIDTH), lambda s, c=c: (s, c))
    return pl.pallas_call(
        functools.partial(_ml_prompt_kernel, chunk=chunk),
        grid=(t // chunk,),
        in_specs=[col(0), col(1), col(2), col(3),
                  pl.BlockSpec((chunk, LANES), lambda s: (s, 0)),
                  pl.BlockSpec((2 * N_HEADS, chunk), lambda s: (0, s)),
                  pl.BlockSpec((1, HEAD_DIM), lambda s: (0, 0)),
                  pl.BlockSpec((chunk, chunk), lambda s: (0, 0)),
                  pl.BlockSpec((chunk, chunk), lambda s: (0, 0))],
        out_specs=[pl.BlockSpec((chunk, WIDTH), lambda s: (s, 0)),
                   pl.BlockSpec((N_HEADS, HEAD_DIM, HEAD_DIM), lambda s: (0, 0, 0)),
                   pl.BlockSpec((N_HEADS, HEAD_DIM), lambda s: (0, 0)),
                   pl.BlockSpec((N_HEADS, LANES), lambda s: (0, 0))],
        out_shape=[jax.ShapeDtypeStruct((t, WIDTH), BF16),
                   jax.ShapeDtypeStruct((N_HEADS, HEAD_DIM, HEAD_DIM), F32),
                   jax.ShapeDtypeStruct((N_HEADS, HEAD_DIM), F32),
                   jax.ShapeDtypeStruct((N_HEADS, LANES), F32)],
        compiler_params=_params(1),
        name="mlstm_prompt",
    )(ml, ml, ml, ml, gates, gates_t, g_mh.reshape(1, HEAD_DIM), tri, tri.T)


def _ml_sample_kernel(x_ref, g_ref, m_ref, c_ref, n_ref, gmh_ref, hm_ref, cn_ref, nn_ref, mn_ref):
    for b in range(x_ref.shape[0]):
        _ml_sample_one(b, x_ref, g_ref, m_ref, c_ref, n_ref, gmh_ref, hm_ref, cn_ref, nn_ref, mn_ref)


def _ml_sample_one(b, x_ref, g_ref, m_ref, c_ref, n_ref, gmh_ref, hm_ref, cn_ref, nn_ref, mn_ref):
    x = x_ref[b]
    q = x[0:N_HEADS]
    k = x[N_HEADS:2 * N_HEADS] * QK_SCALE
    v = x[2 * N_HEADS:3 * N_HEADS]
    og = x[3 * N_HEADS:4 * N_HEADS]
    gates = g_ref[b]
    li = gates[0:N_HEADS]
    lf = _log_sigmoid(gates[N_HEADS:2 * N_HEADS])
    m_old = m_ref[b]
    n_old = n_ref[b]

    inter = lf + m_old
    m_new = jnp.maximum(inter, li)
    s = jnp.sum(q * k, axis=-1, keepdims=True) * jnp.exp(li - m_new)
    w_inter = jnp.exp(inter - m_new)
    qb = q.astype(BF16)
    rowh = lax.broadcasted_iota(jnp.int32, (N_HEADS, HEAD_DIM), 0)
    cq = jnp.zeros((N_HEADS, HEAD_DIM), F32)
    for h in range(N_HEADS):
        r = lax.dot_general(qb, c_ref[b, h].astype(BF16), NT_DIMS, preferred_element_type=F32)
        cq = jnp.where(rowh == h, r, cq)
    num = s * v + w_inter * cq
    den = s + w_inter * jnp.sum(n_old * q, axis=-1, keepdims=True)
    hh = num / jnp.maximum(jnp.abs(den), jnp.exp(-m_new))
    hm_ref[b] = (_rms(hh, gmh_ref[...]) * _sigmoid(og)).astype(hm_ref.dtype)

    wk = jnp.exp(li - m_new)
    decay = w_inter
    nn_ref[b] = decay * n_old + wk * k
    mn_ref[b] = m_new
    wv = wk * v
    wvt = jnp.concatenate([wv, jnp.zeros((LANES - N_HEADS, HEAD_DIM), F32)], axis=0).T
    for h in range(N_HEADS):
        cn_ref[b, h] = decay[h:h + 1, :] * c_ref[b, h] + wvt[:, h:h + 1] * k[h:h + 1, :]


def _ml_sample(ml, gates, state_m, state_c, state_n, g_mh, group):
    nb = ml.shape[0]
    x = ml.reshape(nb, 4 * N_HEADS, HEAD_DIM)
    g = gates.reshape(nb, 2 * N_HEADS, 1)
    m = state_m.reshape(nb, N_HEADS, 1)
    b3 = lambda *shape: pl.BlockSpec((group,) + shape, lambda b: (b,) + (0,) * len(shape))
    return pl.pallas_call(
        _ml_sample_kernel,
        grid=(nb // group,),
        in_specs=[b3(4 * N_HEADS, HEAD_DIM), b3(2 * N_HEADS, 1), b3(N_HEADS, 1),
                  b3(N_HEADS, HEAD_DIM, HEAD_DIM), b3(N_HEADS, HEAD_DIM),
                  pl.BlockSpec((1, HEAD_DIM), lambda b: (0, 0))],
        out_specs=[b3(N_HEADS, HEAD_DIM), b3(N_HEADS, HEAD_DIM, HEAD_DIM), b3(N_HEADS, HEAD_DIM), b3(N_HEADS, 1)],
        out_shape=[jax.ShapeDtypeStruct((nb, N_HEADS, HEAD_DIM), BF16),
                   jax.ShapeDtypeStruct((nb, N_HEADS, HEAD_DIM, HEAD_DIM), F32),
                   jax.ShapeDtypeStruct((nb, N_HEADS, HEAD_DIM), F32),
                   jax.ShapeDtypeStruct((nb, N_HEADS, 1), F32)],
        compiler_params=_params(1),
        name="mlstm_sample",
    )(x, g, m, state_c, state_n, g_mh.reshape(1, HEAD_DIM))


def _dense_front(x, w, tm, q_scale, q_dtype, kv_dtypes):
    xn, gates = _norm_gate(x, w["g_mix"], w["wg_hi"], w["wg_lo"], w["bg"], min(tm, 512))
    d = x.shape[1]
    gq = (w["g_q"].reshape(1, HEAD_DIM), (1, HEAD_DIM), lambda i, j: (0, 0))
    gk = (w["g_k"].reshape(1, HEAD_DIM), (1, HEAD_DIM), lambda i, j: (0, 0))
    tn = 1024
    w_in_t = w["w_in_t"]
    (q,) = _matmul("proj_q", [xn], [(w_in_t, d, 0, 0)], [gq],
                   functools.partial(_ep_headnorm, scale=q_scale), [q_dtype], WIDTH, tm, tn, True)
    k = _matmul("proj_k", [xn], [(w_in_t, d, 0, WIDTH // tn)], [gk],
                functools.partial(_ep_headnorm, scale=1.0), kv_dtypes, WIDTH, tm, tn, True)
    v = _matmul("proj_v", [xn], [(w_in_t, d, 0, 2 * WIDTH // tn)], [], _ep_plain, kv_dtypes, WIDTH, tm, tn, True)
    (ml,) = _matmul("proj_ml", [xn], [(w_in_t, d, 0, 3 * WIDTH // tn)], [], _ep_plain, [F32], 4 * WIDTH, tm, tn,
                    True)
    return q, k, v, ml, gates


def _dense_back(x, sb, hm, p, w, tm):
    m, d = x.shape
    tn = 1024
    res = lambda arr: (arr, (tm, tn), lambda i, j: (i, j))
    (h1,) = _matmul("out_proj", [sb, hm], [(w["w_out"], WIDTH, 0, 0), (w["w_out"], WIDTH, 1, 0)], [res(x)],
                    _ep_residual, [F32], d, tm, tn)
    hn = _norm(h1, w["g_ffn"], min(tm, 512))
    d_ff = w["w_up"].shape[1]
    (act,) = _matmul("ffn_up", [hn], [(w["w_up"], d, 0, 0)], [], _ep_relu2, [BF16], d_ff, tm, tn)
    tm2, tn2 = min(tm, 512), 512
    res2 = lambda arr: (arr, (tm2, tn2), lambda i, j: (i, j))
    (h2,) = _matmul("ffn_down", [act], [(w["w_down"], d_ff, 0, 0)], [res2(h1)], _ep_residual, [F32], d, tm2, tn2)
    hn2 = _norm(h2, w["g_ple"], min(tm, 512))
    ple = p.shape[1]
    extras = [(p, (tm, ple), lambda i, j: (i, 0)),
              (w["w_ple"], (ple, tn), lambda i, j: (0, j)),
              res(h2)]
    (y,) = _matmul("ple_gate", [hn2], [(w["w_pg"], d, 0, 0)], extras, _ep_ple, [F32], d, tm, tn)
    return y


def kernel(x_prompt, x_sample, cache_k, cache_v, state_C, state_n, state_m, page_table, p_prompt, p_sample,
           g_mix, w_in, b_gate, b_sb, g_q, g_k, g_mh, w_out, g_ffn, w_up, w_down, g_ple, w_ple, w_pg):
    depth = w_in.shape[0]
    assert depth == 1 and x_prompt.shape[0] == 1 and x_sample.shape[1] == 1
    t, d = x_prompt.shape[1], x_prompt.shape[2]
    nb = x_sample.shape[0]
    n_proj = w_in.shape[2] - 2 * N_HEADS

    wg = jnp.pad(w_in[0, :, n_proj:], ((0, 0), (0, LANES - 2 * N_HEADS)))
    wg_hi = wg.astype(BF16)
    w = {
        "g_mix": g_mix[0], "g_q": g_q[0], "g_k": g_k[0], "g_ffn": g_ffn[0], "g_ple": g_ple[0],
        "wg_hi": wg_hi, "wg_lo": (wg - wg_hi.astype(F32)).astype(BF16),
        "bg": jnp.pad(b_gate[0], (0, LANES - 2 * N_HEADS)).reshape(1, LANES),
        "w_in_t": jnp.swapaxes(w_in[0], 0, 1).astype(BF16), "w_out": w_out[0].astype(BF16), "w_up": w_up[0].astype(BF16),
        "w_down": w_down[0].astype(BF16), "w_ple": w_ple[0].astype(BF16), "w_pg": w_pg[0].astype(BF16),
    }

    xp = x_prompt[0]
    xs = x_sample[:, 0]
    q, (k, k_bf), (v, v_bf), ml, gates = _dense_front(xp, w, 1024, QK_SCALE * LOG2E, BF16, [F32, BF16])
    qs, (ks,), (vs,), mls, gates_s = _dense_front(xs, w, nb, 1.0, F32, [F32])
    sb, sbs = _sb_attention(q, k_bf, v_bf, b_sb[0], 512, qs.reshape(nb, N_HEADS, HEAD_DIM),
                            cache_k, cache_v, page_table)
    chunk = 256
    gates_t = gates[:, :2 * N_HEADS].T
    hm, c_p, n_p, m_p = _ml_prompt(ml, gates, gates_t, g_mh[0], chunk)
    y_p = _dense_back(xp, sb, hm, p_prompt[0, 0], w, 1024)
    hms, c_s, n_s, m_s = _ml_sample(mls, gates_s[:, :2 * N_HEADS], state_m[0], state_C[0], state_n[0], g_mh[0], 8)
    y_s = _dense_back(xs, sbs.reshape(nb, WIDTH), hms.reshape(nb, WIDTH), p_sample[0, :, 0], w, nb)

    hd = (N_HEADS, HEAD_DIM)
    return (y_p[None], y_s[:, None],
            k.reshape((1, 1, t) + hd), v.reshape((1, 1, t) + hd),
            c_p[None, None], n_p[None, None], m_p[:, 0][None, None],
            ks.reshape((1, nb, 1) + hd), vs.reshape((1, nb, 1) + hd),
            c_s[None], n_s[None], m_s.reshape(1, nb, N_HEADS))
```

```python
import functools
import math

import jax
import jax.numpy as jnp
from jax import lax
from jax.experimental import pallas as pl
from jax.experimental.pallas import tpu as pltpu

F32 = jnp.float32
BF16 = jnp.bfloat16

HEAD_DIM = 128
N_HEADS = 8
WIDTH = N_HEADS * HEAD_DIM
RMS_EPS = 1e-6
QK_SCALE = HEAD_DIM ** -0.5
LOG2E = 1.0 / math.log(2.0)
LANES = 128
VMEM_LIMIT = 56 * 1024 * 1024

NT_DIMS = (((1,), (1,)), ((), ()))
TN_DIMS = (((0,), (0,)), ((), ()))


def _params(n_grid_dims):
    return pltpu.CompilerParams(
        dimension_semantics=("arbitrary",) * n_grid_dims,
        vmem_limit_bytes=VMEM_LIMIT)


def _log_sigmoid(x):
    return jnp.minimum(x, 0.0) - jnp.log1p(jnp.exp(-jnp.abs(x)))


def _sigmoid(x):
    return 1.0 / (1.0 + jnp.exp(-x))


def _split2(x):
    hi = x.astype(BF16)
    lo = (x - hi.astype(F32)).astype(BF16)
    return hi, lo


def _split3(x):
    x1 = x.astype(BF16)
    r1 = x - x1.astype(F32)
    x2 = r1.astype(BF16)
    x3 = (r1 - x2.astype(F32)).astype(BF16)
    return x1, x2, x3


def _dot(a, b):
    return jnp.dot(a, b, preferred_element_type=F32)


def _rms(x, g):
    return x * lax.rsqrt(jnp.mean(x * x, axis=-1, keepdims=True) + RMS_EPS) * g


def _norm_kernel(x_ref, g_ref, o_ref):
    o_ref[...] = _rms(x_ref[...], g_ref[...]).astype(o_ref.dtype)


def _norm_gate_kernel(x_ref, g_ref, wh_ref, wl_ref, b_ref, o_ref, gate_ref):
    y = _rms(x_ref[...], g_ref[...])
    yh, yl = _split2(y)
    o_ref[...] = yh
    wh = wh_ref[...]
    gate_ref[...] = _dot(yh, wh) + _dot(yh, wl_ref[...]) + _dot(yl, wh) + b_ref[...]


def _norm(x, g, tm):
    m, d = x.shape
    return pl.pallas_call(
        _norm_kernel,
        grid=(m // tm,),
        in_specs=[pl.BlockSpec((tm, d), lambda i: (i, 0)),
                  pl.BlockSpec((1, d), lambda i: (0, 0))],
        out_specs=pl.BlockSpec((tm, d), lambda i: (i, 0)),
        out_shape=jax.ShapeDtypeStruct((m, d), BF16),
        compiler_params=_params(1),
        name="rmsnorm",
    )(x, g.reshape(1, d))


def _norm_gate(x, g, wg_hi, wg_lo, bg, tm):
    m, d = x.shape
    return pl.pallas_call(
        _norm_gate_kernel,
        grid=(m // tm,),
        in_specs=[pl.BlockSpec((tm, d), lambda i: (i, 0)),
                  pl.BlockSpec((1, d), lambda i: (0, 0)),
                  pl.BlockSpec((d, LANES), lambda i: (0, 0)),
                  pl.BlockSpec((d, LANES), lambda i: (0, 0)),
                  pl.BlockSpec((1, LANES), lambda i: (0, 0))],
        out_specs=[pl.BlockSpec((tm, d), lambda i: (i, 0)),
                   pl.BlockSpec((tm, LANES), lambda i: (i, 0))],
        out_shape=[jax.ShapeDtypeStruct((m, d), BF16),
                   jax.ShapeDtypeStruct((m, LANES), F32)],
        compiler_params=_params(1),
        name="rmsnorm_gates",
    )(x, g.reshape(1, d), wg_hi, wg_lo, bg)


def _mm_kernel(*refs, n_a, n_e, epilogue, w_transposed):
    a_refs = refs[:n_a]
    w_refs = refs[n_a:2 * n_a]
    e_refs = refs[2 * n_a:2 * n_a + n_e]
    o_refs = refs[2 * n_a + n_e:]
    acc = None
    for a_ref, w_ref in zip(a_refs, w_refs):
        if w_transposed:
            d = lax.dot_general(a_ref[...], w_ref[...], NT_DIMS, preferred_element_type=F32)
        else:
            d = _dot(a_ref[...], w_ref[...])
        acc = d if acc is None else acc + d
    epilogue(acc, e_refs, o_refs)


def _matmul(name, a_list, w_list, extras, epilogue, out_dtypes, n, tm, tn, w_transposed=False):
    m = a_list[0].shape[0]
    in_specs, operands = [], []
    for a in a_list:
        in_specs.append(pl.BlockSpec((tm, a.shape[1]), lambda i, j: (i, 0)))
        operands.append(a)
    for w, k, rb, cb in w_list:
        if w_transposed:
            in_specs.append(pl.BlockSpec((tn, k), lambda i, j, rb=rb, cb=cb: (j + cb, rb)))
        else:
            in_specs.append(pl.BlockSpec((k, tn), lambda i, j, rb=rb, cb=cb: (rb, j + cb)))
        operands.append(w)
    for arr, blk, imap in extras:
        in_specs.append(pl.BlockSpec(blk, imap))
        operands.append(arr)
    out_specs = [pl.BlockSpec((tm, tn), lambda i, j: (i, j)) for _ in out_dtypes]
    out_shape = [jax.ShapeDtypeStruct((m, n), dt) for dt in out_dtypes]
    return pl.pallas_call(
        functools.partial(_mm_kernel, n_a=len(a_list), n_e=len(extras), epilogue=epilogue,
                          w_transposed=w_transposed),
        grid=(m // tm, n // tn),
        in_specs=in_specs,
        out_specs=out_specs,
        out_shape=out_shape,
        compiler_params=_params(2),
        name=name,
    )(*operands)


def _ep_plain(acc, e_refs, o_refs):
    for o_ref in o_refs:
        o_ref[...] = acc.astype(o_ref.dtype)


def _ep_headnorm(acc, e_refs, o_refs, *, scale):
    g = e_refs[0][...]
    for c in range(acc.shape[1] // HEAD_DIM):
        blk = acc[:, c * HEAD_DIM:(c + 1) * HEAD_DIM]
        y = _rms(blk, g)
        if scale != 1.0:
            y = y * scale
        for o_ref in o_refs:
            o_ref[:, c * HEAD_DIM:(c + 1) * HEAD_DIM] = y.astype(o_ref.dtype)


def _ep_residual(acc, e_refs, o_refs):
    o_refs[0][...] = e_refs[0][...] + acc


def _ep_relu2(acc, e_refs, o_refs):
    r = jnp.maximum(acc, 0.0)
    o_refs[0][...] = (r * r).astype(o_refs[0].dtype)


def _ep_ple(acc, e_refs, o_refs):
    p_ref, wple_ref, h_ref = e_refs
    emb = _dot(p_ref[...].astype(BF16), wple_ref[...])
    o_refs[0][...] = h_ref[...] + emb * _sigmoid(acc)


def _softplus(z):
    return jnp.maximum(z, 0.0) + jnp.log(1.0 + jnp.exp(-jnp.abs(z)))


def _split_trunc(x):
    hi_f = lax.bitcast_convert_type(lax.bitcast_convert_type(x, jnp.uint32) & jnp.uint32(0xFFFF0000), F32)
    return hi_f.astype(BF16), (x - hi_f).astype(BF16)


def _softplus2(zz):
    neg_abs = lax.bitcast_convert_type(lax.bitcast_convert_type(zz, jnp.uint32) | jnp.uint32(0x80000000), F32)
    return jnp.maximum(zz, 0.0) + jnp.log(1.0 + jnp.exp2(neg_abs)) * LOG2E


def _sb_prompt_body(h, i, b_ref, q_ref, k_ref, v_ref, p_ref, o_ref, c_ref, acc_ref, tq):
    bias = b_ref[h]
    q = q_ref[...]
    p2 = p_ref[...]

    def tile(first_key, width, masked):
        start = pl.multiple_of(first_key, tq)
        kt = k_ref[pl.ds(start, width), :]
        vt = v_ref[pl.ds(start, width), :]
        zz = lax.dot_general(q, kt, NT_DIMS, preferred_element_type=F32) + bias
        sp = _softplus2(zz)
        if masked:
            mask = (lax.broadcasted_iota(jnp.int32, (tq, width), 1)
                    < lax.broadcasted_iota(jnp.int32, (tq, width), 0))
            sp = jnp.where(mask, sp, 0.0)
        hi, lo = _split_trunc(sp)
        c = c_ref[...]
        n_sub = width // LANES
        es = [None] * n_sub
        for u in reversed(range(n_sub)):
            sl = slice(u * LANES, (u + 1) * LANES)
            r = _dot(jnp.concatenate([hi[:, sl], lo[:, sl]], axis=1), p2)
            es[u] = jnp.exp2(zz[:, sl] + r[:, :LANES] + c)
            c = c + r[:, LANES:]
        c_ref[...] = c
        a = jnp.concatenate(es, axis=1)
        if masked:
            a = jnp.where(mask, a, 0.0)
        acc_ref[...] += _dot(a.astype(BF16), vt)

    c_ref[...] = jnp.zeros_like(c_ref)
    acc_ref[...] = jnp.zeros_like(acc_ref)
    tile(i * tq, tq, True)
    odd = i % 2

    @pl.when(odd == 1)
    def _():
        tile((i - 1) * tq, tq, False)

    def body(s, carry):
        tile((i - odd - 2 - 2 * s) * tq, 2 * tq, False)
        return carry

    lax.fori_loop(0, i // 2, body, 0)
    o_ref[...] = acc_ref[...].astype(o_ref.dtype)


def _cumsum_matrix(group, inclusive):
    lp = lax.broadcasted_iota(jnp.int32, (2 * LANES, 2 * LANES), 0) % LANES
    l = lax.broadcasted_iota(jnp.int32, (2 * LANES, 2 * LANES), 1)
    same = (lp % group) == (l % group)
    later = ((lp >= l) if inclusive else (lp > l)) | (l >= LANES)
    return jnp.where(same & later, -1.0, 0.0).astype(BF16)


def _sb_sample_body(q_ref, b_ref, p_ref, u_ref, k_refs, v_refs, o_ref, z_ref):
    n_pages = len(k_refs)
    rows = k_refs[0].shape[2] * N_HEADS
    n_tiles = rows // LANES

    q8 = q_ref[0] * QK_SCALE
    zero = jnp.zeros_like(q8)
    q16 = jnp.concatenate([jnp.concatenate([q8, zero], axis=1),
                           jnp.concatenate([zero, q8], axis=1)], axis=0).astype(BF16)
    diag = (lax.broadcasted_iota(jnp.int32, (N_HEADS, rows), 0)
            == lax.broadcasted_iota(jnp.int32, (N_HEADS, rows), 1) % N_HEADS)

    def pick(x):
        return jnp.sum(jnp.where(diag, x, 0.0), axis=0, keepdims=True)

    for pp in range(n_pages // 2):
        ka = k_refs[2 * pp][0, 0].reshape(rows, HEAD_DIM).astype(BF16)
        kb = k_refs[2 * pp + 1][0, 0].reshape(rows, HEAD_DIM).astype(BF16)
        res = lax.dot_general(q16, jnp.concatenate([ka, kb], axis=1), NT_DIMS, preferred_element_type=F32)
        z_ref[2 * pp:2 * pp + 1, :] = pick(res[:N_HEADS])
        z_ref[2 * pp + 1:2 * pp + 2, :] = pick(res[N_HEADS:])

    z = z_ref[...] + b_ref[...]
    sp = _softplus(z)
    hi, lo = _split_trunc(sp)
    p2 = p_ref[...]
    within, carry = [None] * n_tiles, [None] * n_tiles
    run = jnp.zeros((n_pages, LANES), F32)
    for c in reversed(range(n_tiles)):
        sl = slice(c * LANES, (c + 1) * LANES)
        r = _dot(jnp.concatenate([hi[:, sl], lo[:, sl]], axis=1), p2)
        within[c] = r[:, :LANES]
        carry[c] = run
        run = run + r[:, LANES:]
    later_pages = sum(_dot(u_ref[...], x) for x in _split3(run))
    t = z - sp
    a = jnp.concatenate([jnp.exp(t[:, c * LANES:(c + 1) * LANES] + within[c] + carry[c] + later_pages)
                         for c in range(n_tiles)], axis=1)

    acc = jnp.zeros((N_HEADS, HEAD_DIM), F32)
    for pp in range(n_pages // 2):
        sel = jnp.concatenate(
            [jnp.where(diag, jnp.broadcast_to(a[2 * pp:2 * pp + 1], (N_HEADS, rows)), 0.0),
             jnp.where(diag, jnp.broadcast_to(a[2 * pp + 1:2 * pp + 2], (N_HEADS, rows)), 0.0)],
            axis=0).astype(BF16)
        va = v_refs[2 * pp][0, 0].reshape(rows, HEAD_DIM).astype(BF16)
        vb = v_refs[2 * pp + 1][0, 0].reshape(rows, HEAD_DIM).astype(BF16)
        r = _dot(sel, jnp.concatenate([va, vb], axis=1))
        acc = acc + r[:N_HEADS, :HEAD_DIM] + r[N_HEADS:, HEAD_DIM:]
    o_ref[0] = acc.astype(o_ref.dtype)


def _sb_attention_kernel(pt_ref, bp_ref, qp_ref, kp_ref, vp_ref, pp_ref, qs_ref, bs_ref, ps_ref, u_ref, *refs,
                         n_pages, tq):
    del pt_ref
    k_refs = refs[:n_pages]
    v_refs = refs[n_pages:2 * n_pages]
    op_ref, os_ref, c_ref, acc_ref, z_ref = refs[2 * n_pages:]
    _sb_sample_body(qs_ref, bs_ref, ps_ref, u_ref, k_refs, v_refs, os_ref, z_ref)
    _sb_prompt_body(pl.program_id(0), pl.program_id(1), bp_ref, qp_ref, kp_ref, vp_ref, pp_ref, op_ref,
                    c_ref, acc_ref, tq)


def _sb_attention(q, k, v, b_sb, tq, qs, cache_k, cache_v, page_table):
    t = q.shape[0]
    n_blocks = t // tq
    nb, n_pages = page_table.shape
    assert nb == N_HEADS * n_blocks, "one sample sequence per (head, query block) grid step"
    page = cache_k.shape[2]
    rows = page * N_HEADS
    b_row = jnp.tile(b_sb, page).reshape(1, rows)
    pi = lax.broadcasted_iota(jnp.int32, (n_pages, n_pages), 0)
    pj = lax.broadcasted_iota(jnp.int32, (n_pages, n_pages), 1)
    later = (pj > pi).astype(BF16)
    seq = lambda h, i: h * n_blocks + i
    kv_specs = [pl.BlockSpec((1, 1, page, N_HEADS, HEAD_DIM),
                             lambda h, i, pt, j=j: (0, pt[seq(h, i), j], 0, 0, 0))
                for j in range(n_pages)]
    const = lambda shape: pl.BlockSpec(shape, lambda h, i, pt: (0,) * len(shape))
    grid_spec = pltpu.PrefetchScalarGridSpec(
        num_scalar_prefetch=1,
        grid=(N_HEADS, n_blocks),
        in_specs=[pl.BlockSpec(memory_space=pltpu.SMEM),
                  pl.BlockSpec((tq, HEAD_DIM), lambda h, i, pt: (i, h)),
                  pl.BlockSpec((t, HEAD_DIM), lambda h, i, pt: (0, h)),
                  pl.BlockSpec((t, HEAD_DIM), lambda h, i, pt: (0, h)),
                  const((2 * LANES, 2 * LANES)),
                  pl.BlockSpec((1, N_HEADS, HEAD_DIM), lambda h, i, pt: (seq(h, i), 0, 0)),
                  const((1, rows)),
                  const((2 * LANES, 2 * LANES)),
                  const((n_pages, n_pages))] + kv_specs + kv_specs,
        out_specs=[pl.BlockSpec((tq, HEAD_DIM), lambda h, i, pt: (i, h)),
                   pl.BlockSpec((1, N_HEADS, HEAD_DIM), lambda h, i, pt: (seq(h, i), 0, 0))],
        scratch_shapes=[pltpu.VMEM((tq, LANES), F32), pltpu.VMEM((tq, HEAD_DIM), F32),
                        pltpu.VMEM((n_pages, rows), F32)])
    return pl.pallas_call(
        functools.partial(_sb_attention_kernel, n_pages=n_pages, tq=tq),
        grid_spec=grid_spec,
        out_shape=[jax.ShapeDtypeStruct((t, WIDTH), BF16),
                   jax.ShapeDtypeStruct((nb, N_HEADS, HEAD_DIM), BF16)],
        compiler_params=_params(2),
        name="sb_attention",
    )(page_table, b_sb * LOG2E, q, k, v, _cumsum_matrix(1, True),
      qs, b_row, _cumsum_matrix(N_HEADS, False), later, *([cache_k] * n_pages), *([cache_v] * n_pages))


def _ml_prompt_kernel(q_ref, k_ref, v_ref, og_ref, g_ref, gt_ref, gmh_ref, tril_ref, triu_ref, sel_ref,
                      hm_ref, c_ref, n_ref, m_ref, *, chunk):
    step = pl.program_id(0)

    @pl.when(step == 0)
    def _():
        c_ref[...] = jnp.zeros_like(c_ref)
        n_ref[...] = jnp.zeros_like(n_ref)
        m_ref[...] = jnp.zeros_like(m_ref)

    g = g_ref[...]
    gt = gt_ref[...]
    lane = lax.broadcasted_iota(jnp.int32, g.shape, 1)
    rowi = lax.broadcasted_iota(jnp.int32, gt.shape, 0)
    gl = jnp.where(lane >= N_HEADS, _log_sigmoid(g), g)
    gtl = jnp.where(rowi >= N_HEADS, _log_sigmoid(gt), gt)
    tril = tril_ref[...]
    triu = triu_ref[...]
    a_col = sum(_dot(tril, x) for x in _split3(gl))
    a_row = sum(_dot(x, triu) for x in _split3(gtl))
    xs = _split3(jnp.concatenate([gl, a_col], axis=1))
    tt = lax.broadcasted_iota(jnp.int32, (chunk, chunk), 0)
    ss = lax.broadcasted_iota(jnp.int32, (chunk, chunk), 1)
    causal = ss <= tt
    gmh = gmh_ref[...]
    ones = jnp.ones((chunk, HEAD_DIM), BF16)
    ones_sq = jnp.ones((HEAD_DIM, HEAD_DIM), BF16)

    def wide(x):
        return jnp.concatenate([x] * (chunk // LANES), axis=1)

    for h in range(N_HEADS):
        sl = slice(h * HEAD_DIM, (h + 1) * HEAD_DIM)
        lb = sum(_dot(x, sel_ref[h]) for x in xs)
        li_rep = lb[:, :LANES]
        b_rep = lb[:, LANES:]
        b_row = a_row[N_HEADS + h:N_HEADS + h + 1, :]
        li_row = gtl[h:h + 1, :]
        m_old = m_ref[h:h + 1, :]
        b_end = b_rep[chunk - 1:chunk, :]

        dmat = wide(b_rep) - (b_row - li_row)
        inter = b_rep + m_old
        dmax = jnp.max(jnp.where(causal, dmat, -jnp.inf), axis=-1, keepdims=True)
        m_row = jnp.maximum(inter, dmax)
        e = jnp.where(causal, jnp.exp(dmat - wide(m_row)), 0.0)
        w_inter = jnp.exp(inter - m_row)

        qc = q_ref[:, sl]
        kc = k_ref[:, sl] * QK_SCALE
        vc = v_ref[:, sl]
        qb = qc.astype(BF16)
        kb = kc.astype(BF16)
        s = lax.dot_general(qb, kb, NT_DIMS, preferred_element_type=F32) * e
        c_old = c_ref[h]
        n_old = n_ref[h:h + 1, :]
        r1 = _dot(s.astype(BF16), jnp.concatenate([vc.astype(BF16), ones], axis=1))
        cn = jnp.concatenate([c_old, jnp.broadcast_to(n_old, (HEAD_DIM, HEAD_DIM))], axis=0).astype(BF16)
        r2 = lax.dot_general(qb, cn, NT_DIMS, preferred_element_type=F32)
        num = r1[:, :HEAD_DIM] + w_inter * r2[:, :HEAD_DIM]
        den = r1[:, HEAD_DIM:] + w_inter * r2[:, HEAD_DIM:]
        hh = num / jnp.maximum(jnp.abs(den), jnp.exp(-m_row))
        sq_hi, sq_lo = _split2(hh * hh)
        ms = (_dot(sq_hi, ones_sq) + _dot(sq_lo, ones_sq)) * (1.0 / HEAD_DIM)
        hn = hh * lax.rsqrt(ms + RMS_EPS) * gmh
        hm_ref[:, sl] = (hn * _sigmoid(og_ref[:, sl])).astype(hm_ref.dtype)

        gw = b_end - b_rep + li_rep
        m_new = jnp.maximum(b_end + m_old, jnp.max(gw, axis=0, keepdims=True))
        wk = jnp.exp(gw - m_new)
        decay = jnp.exp(b_end + m_old - m_new)
        c_ref[h] = decay * c_old + lax.dot_general((wk * vc).astype(BF16), kb, TN_DIMS,
                                                   preferred_element_type=F32)
        n_ref[h:h + 1, :] = decay * n_old + jnp.sum(wk * kc, axis=0, keepdims=True)
        m_ref[h:h + 1, :] = m_new


def _gate_select_matrices():
    shape = (N_HEADS, 2 * LANES, 2 * LANES)
    h = lax.broadcasted_iota(jnp.int32, shape, 0)
    r = lax.broadcasted_iota(jnp.int32, shape, 1)
    c = lax.broadcasted_iota(jnp.int32, shape, 2)
    return jnp.where(c < LANES, r == h, r == LANES + N_HEADS + h).astype(BF16)


def _ml_prompt(ml, gates, gates_t, g_mh, chunk):
    t = ml.shape[0]
    tri = jnp.tril(jnp.ones((chunk, chunk), F32)).astype(BF16)
    col = lambda c: pl.BlockSpec((chunk, WIDTH), lambda s, c=c: (s, c))
    return pl.pallas_call(
        functools.partial(_ml_prompt_kernel, chunk=chunk),
        grid=(t // chunk,),
        in_specs=[col(0), col(1), col(2), col(3),
                  pl.BlockSpec((chunk, LANES), lambda s: (s, 0)),
                  pl.BlockSpec((2 * N_HEADS, chunk), lambda s: (0, s)),
                  pl.BlockSpec((1, HEAD_DIM), lambda s: (0, 0)),
                  pl.BlockSpec((chunk, chunk), lambda s: (0, 0)),
                  pl.BlockSpec((chunk, chunk), lambda s: (0, 0)),
                  pl.BlockSpec((N_HEADS, 2 * LANES, 2 * LANES), lambda s: (0, 0, 0))],
        out_specs=[pl.BlockSpec((chunk, WIDTH), lambda s: (s, 0)),
                   pl.BlockSpec((N_HEADS, HEAD_DIM, HEAD_DIM), lambda s: (0, 0, 0)),
                   pl.BlockSpec((N_HEADS, HEAD_DIM), lambda s: (0, 0)),
                   pl.BlockSpec((N_HEADS, LANES), lambda s: (0, 0))],
        out_shape=[jax.ShapeDtypeStruct((t, WIDTH), BF16),
                   jax.ShapeDtypeStruct((N_HEADS, HEAD_DIM, HEAD_DIM), F32),
                   jax.ShapeDtypeStruct((N_HEADS, HEAD_DIM), F32),
                   jax.ShapeDtypeStruct((N_HEADS, LANES), F32)],
        compiler_params=_params(1),
        name="mlstm_prompt",
    )(ml, ml, ml, ml, gates, gates_t, g_mh.reshape(1, HEAD_DIM), tri, tri.T, _gate_select_matrices())


def _ml_sample_kernel(x_ref, g_ref, m_ref, c_ref, n_ref, gmh_ref, hm_ref, cn_ref, nn_ref, mn_ref):
    for b in range(x_ref.shape[0]):
        _ml_sample_one(b, x_ref, g_ref, m_ref, c_ref, n_ref, gmh_ref, hm_ref, cn_ref, nn_ref, mn_ref)


def _ml_sample_one(b, x_ref, g_ref, m_ref, c_ref, n_ref, gmh_ref, hm_ref, cn_ref, nn_ref, mn_ref):
    x = x_ref[b]
    q = x[0:N_HEADS]
    k = x[N_HEADS:2 * N_HEADS] * QK_SCALE
    v = x[2 * N_HEADS:3 * N_HEADS]
    og = x[3 * N_HEADS:4 * N_HEADS]
    gates = g_ref[b]
    li = gates[0:N_HEADS]
    lf = _log_sigmoid(gates[N_HEADS:2 * N_HEADS])
    m_old = m_ref[b]
    n_old = n_ref[b]

    inter = lf + m_old
    m_new = jnp.maximum(inter, li)
    s = jnp.sum(q * k, axis=-1, keepdims=True) * jnp.exp(li - m_new)
    w_inter = jnp.exp(inter - m_new)
    qb = q.astype(BF16)
    rowh = lax.broadcasted_iota(jnp.int32, (N_HEADS, HEAD_DIM), 0)
    cq = jnp.zeros((N_HEADS, HEAD_DIM), F32)
    for h in range(N_HEADS):
        r = lax.dot_general(qb, c_ref[b, h].astype(BF16), NT_DIMS, preferred_element_type=F32)
        cq = jnp.where(rowh == h, r, cq)
    num = s * v + w_inter * cq
    den = s + w_inter * jnp.sum(n_old * q, axis=-1, keepdims=True)
    hh = num / jnp.maximum(jnp.abs(den), jnp.exp(-m_new))
    hm_ref[b] = (_rms(hh, gmh_ref[...]) * _sigmoid(og)).astype(hm_ref.dtype)

    wk = jnp.exp(li - m_new)
    decay = w_inter
    nn_ref[b] = decay * n_old + wk * k
    mn_ref[b] = m_new
    wv = wk * v
    wvt = jnp.concatenate([wv, jnp.zeros((LANES - N_HEADS, HEAD_DIM), F32)], axis=0).T
    for h in range(N_HEADS):
        cn_ref[b, h] = decay[h:h + 1, :] * c_ref[b, h] + wvt[:, h:h + 1] * k[h:h + 1, :]


def _ml_sample(ml, gates, state_m, state_c, state_n, g_mh, group):
    nb = ml.shape[0]
    x = ml.reshape(nb, 4 * N_HEADS, HEAD_DIM)
    g = gates.reshape(nb, 2 * N_HEADS, 1)
    m = state_m.reshape(nb, N_HEADS, 1)
    b3 = lambda *shape: pl.BlockSpec((group,) + shape, lambda b: (b,) + (0,) * len(shape))
    return pl.pallas_call(
        _ml_sample_kernel,
        grid=(nb // group,),
        in_specs=[b3(4 * N_HEADS, HEAD_DIM), b3(2 * N_HEADS, 1), b3(N_HEADS, 1),
                  b3(N_HEADS, HEAD_DIM, HEAD_DIM), b3(N_HEADS, HEAD_DIM),
                  pl.BlockSpec((1, HEAD_DIM), lambda b: (0, 0))],
        out_specs=[b3(N_HEADS, HEAD_DIM), b3(N_HEADS, HEAD_DIM, HEAD_DIM), b3(N_HEADS, HEAD_DIM), b3(N_HEADS, 1)],
        out_shape=[jax.ShapeDtypeStruct((nb, N_HEADS, HEAD_DIM), BF16),
                   jax.ShapeDtypeStruct((nb, N_HEADS, HEAD_DIM, HEAD_DIM), F32),
                   jax.ShapeDtypeStruct((nb, N_HEADS, HEAD_DIM), F32),
                   jax.ShapeDtypeStruct((nb, N_HEADS, 1), F32)],
        compiler_params=_params(1),
        name="mlstm_sample",
    )(x, g, m, state_c, state_n, g_mh.reshape(1, HEAD_DIM))


def _dense_front(x, w, tm, q_scale, q_dtype, kv_dtypes):
    xn, gates = _norm_gate(x, w["g_mix"], w["wg_hi"], w["wg_lo"], w["bg"], min(tm, 512))
    d = x.shape[1]
    gq = (w["g_q"].reshape(1, HEAD_DIM), (1, HEAD_DIM), lambda i, j: (0, 0))
    gk = (w["g_k"].reshape(1, HEAD_DIM), (1, HEAD_DIM), lambda i, j: (0, 0))
    tn = 1024
    w_in_t = w["w_in_t"]
    (q,) = _matmul("proj_q", [xn], [(w_in_t, d, 0, 0)], [gq],
                   functools.partial(_ep_headnorm, scale=q_scale), [q_dtype], WIDTH, tm, tn, True)
    k = _matmul("proj_k", [xn], [(w_in_t, d, 0, WIDTH // tn)], [gk],
                functools.partial(_ep_headnorm, scale=1.0), kv_dtypes, WIDTH, tm, tn, True)
    v = _matmul("proj_v", [xn], [(w_in_t, d, 0, 2 * WIDTH // tn)], [], _ep_plain, kv_dtypes, WIDTH, tm, tn, True)
    (ml,) = _matmul("proj_ml", [xn], [(w_in_t, d, 0, 3 * WIDTH // tn)], [], _ep_plain, [F32], 4 * WIDTH, tm, tn,
                    True)
    return q, k, v, ml, gates


def _dense_back(x, sb, hm, p, w, tm):
    m, d = x.shape
    tn = 1024
    res = lambda arr: (arr, (tm, tn), lambda i, j: (i, j))
    (h1,) = _matmul("out_proj", [sb, hm], [(w["w_out"], WIDTH, 0, 0), (w["w_out"], WIDTH, 1, 0)], [res(x)],
                    _ep_residual, [F32], d, tm, tn)
    hn = _norm(h1, w["g_ffn"], min(tm, 512))
    d_ff = w["w_up"].shape[1]
    (act,) = _matmul("ffn_up", [hn], [(w["w_up"], d, 0, 0)], [], _ep_relu2, [BF16], d_ff, tm, tn)
    tm2, tn2 = tm, 256
    res2 = lambda arr: (arr, (tm2, tn2), lambda i, j: (i, j))
    (h2,) = _matmul("ffn_down", [act], [(w["w_down"], d_ff, 0, 0)], [res2(h1)], _ep_residual, [F32], d, tm2, tn2)
    hn2 = _norm(h2, w["g_ple"], min(tm, 512))
    ple = p.shape[1]
    extras = [(p, (tm, ple), lambda i, j: (i, 0)),
              (w["w_ple"], (ple, tn), lambda i, j: (0, j)),
              res(h2)]
    (y,) = _matmul("ple_gate", [hn2], [(w["w_pg"], d, 0, 0)], extras, _ep_ple, [F32], d, tm, tn)
    return y


def kernel(x_prompt, x_sample, cache_k, cache_v, state_C, state_n, state_m, page_table, p_prompt, p_sample,
           g_mix, w_in, b_gate, b_sb, g_q, g_k, g_mh, w_out, g_ffn, w_up, w_down, g_ple, w_ple, w_pg):
    depth = w_in.shape[0]
    assert depth == 1 and x_prompt.shape[0] == 1 and x_sample.shape[1] == 1
    t, d = x_prompt.shape[1], x_prompt.shape[2]
    nb = x_sample.shape[0]
    n_proj = w_in.shape[2] - 2 * N_HEADS

    wg = jnp.pad(w_in[0, :, n_proj:], ((0, 0), (0, LANES - 2 * N_HEADS)))
    wg_hi = wg.astype(BF16)
    w = {
        "g_mix": g_mix[0], "g_q": g_q[0], "g_k": g_k[0], "g_ffn": g_ffn[0], "g_ple": g_ple[0],
        "wg_hi": wg_hi, "wg_lo": (wg - wg_hi.astype(F32)).astype(BF16),
        "bg": jnp.pad(b_gate[0], (0, LANES - 2 * N_HEADS)).reshape(1, LANES),
        "w_in_t": jnp.swapaxes(w_in[0], 0, 1).astype(BF16), "w_out": w_out[0].astype(BF16), "w_up": w_up[0].astype(BF16),
        "w_down": w_down[0].astype(BF16), "w_ple": w_ple[0].astype(BF16), "w_pg": w_pg[0].astype(BF16),
    }

    xp = x_prompt[0]
    xs = x_sample[:, 0]
    q, (k, k_bf), (v, v_bf), ml, gates = _dense_front(xp, w, 1024, QK_SCALE * LOG2E, BF16, [F32, BF16])
    qs, (ks,), (vs,), mls, gates_s = _dense_front(xs, w, nb, 1.0, F32, [F32])
    sb, sbs = _sb_attention(q, k_bf, v_bf, b_sb[0], 512, qs.reshape(nb, N_HEADS, HEAD_DIM),
                            cache_k, cache_v, page_table)
    chunk = 256
    gates_t = gates[:, :2 * N_HEADS].T
    hm, c_p, n_p, m_p = _ml_prompt(ml, gates, gates_t, g_mh[0], chunk)
    y_p = _dense_back(xp, sb, hm, p_prompt[0, 0], w, 1024)
    hms, c_s, n_s, m_s = _ml_sample(mls, gates_s[:, :2 * N_HEADS], state_m[0], state_C[0], state_n[0], g_mh[0], 8)
    y_s = _dense_back(xs, sbs.reshape(nb, WIDTH), hms.reshape(nb, WIDTH), p_sample[0, :, 0], w, nb)

    hd = (N_HEADS, HEAD_DIM)
    return (y_p[None], y_s[:, None],
            k.reshape((1, 1, t) + hd), v.reshape((1, 1, t) + hd),
            c_p[None, None], n_p[None, None], m_p[:, 0][None, None],
            ks.reshape((1, nb, 1) + hd), vs.reshape((1, nb, 1) + hd),
            c_s[None], n_s[None], m_s.reshape(1, nb, N_HEADS))
```

```python
import functools
import math

import jax
import jax.numpy as jnp
from jax import lax
from jax.experimental import pallas as pl
from jax.experimental.pallas import tpu as pltpu

F32 = jnp.float32
BF16 = jnp.bfloat16

HEAD_DIM = 128
N_HEADS = 8
WIDTH = N_HEADS * HEAD_DIM
RMS_EPS = 1e-6
QK_SCALE = HEAD_DIM ** -0.5
LOG2E = 1.0 / math.log(2.0)
LANES = 128
VMEM_LIMIT = 56 * 1024 * 1024

NT_DIMS = (((1,), (1,)), ((), ()))
TN_DIMS = (((0,), (0,)), ((), ()))


def _params(n_grid_dims):
    return pltpu.CompilerParams(
        dimension_semantics=("arbitrary",) * n_grid_dims,
        vmem_limit_bytes=VMEM_LIMIT)


def _log_sigmoid(x):
    return jnp.minimum(x, 0.0) - jnp.log1p(jnp.exp(-jnp.abs(x)))


def _sigmoid(x):
    return 1.0 / (1.0 + jnp.exp(-x))


def _split2(x):
    hi = x.astype(BF16)
    lo = (x - hi.astype(F32)).astype(BF16)
    return hi, lo


def _split3(x):
    x1 = x.astype(BF16)
    r1 = x - x1.astype(F32)
    x2 = r1.astype(BF16)
    x3 = (r1 - x2.astype(F32)).astype(BF16)
    return x1, x2, x3


def _dot(a, b):
    return jnp.dot(a, b, preferred_element_type=F32)


def _rms(x, g):
    return x * lax.rsqrt(jnp.mean(x * x, axis=-1, keepdims=True) + RMS_EPS) * g


def _norm_kernel(x_ref, g_ref, o_ref):
    o_ref[...] = _rms(x_ref[...], g_ref[...]).astype(o_ref.dtype)


def _norm_gate_kernel(x_ref, g_ref, wh_ref, wl_ref, b_ref, o_ref, gate_ref):
    y = _rms(x_ref[...], g_ref[...])
    yh, yl = _split2(y)
    o_ref[...] = yh
    wh = wh_ref[...]
    gate_ref[...] = _dot(yh, wh) + _dot(yh, wl_ref[...]) + _dot(yl, wh) + b_ref[...]


def _norm(x, g, tm):
    m, d = x.shape
    return pl.pallas_call(
        _norm_kernel,
        grid=(m // tm,),
        in_specs=[pl.BlockSpec((tm, d), lambda i: (i, 0)),
                  pl.BlockSpec((1, d), lambda i: (0, 0))],
        out_specs=pl.BlockSpec((tm, d), lambda i: (i, 0)),
        out_shape=jax.ShapeDtypeStruct((m, d), BF16),
        compiler_params=_params(1),
        name="rmsnorm",
    )(x, g.reshape(1, d))


def _norm_gate(x, g, wg_hi, wg_lo, bg, tm):
    m, d = x.shape
    return pl.pallas_call(
        _norm_gate_kernel,
        grid=(m // tm,),
        in_specs=[pl.BlockSpec((tm, d), lambda i: (i, 0)),
                  pl.BlockSpec((1, d), lambda i: (0, 0)),
                  pl.BlockSpec((d, LANES), lambda i: (0, 0)),
                  pl.BlockSpec((d, LANES), lambda i: (0, 0)),
                  pl.BlockSpec((1, LANES), lambda i: (0, 0))],
        out_specs=[pl.BlockSpec((tm, d), lambda i: (i, 0)),
                   pl.BlockSpec((tm, LANES), lambda i: (i, 0))],
        out_shape=[jax.ShapeDtypeStruct((m, d), BF16),
                   jax.ShapeDtypeStruct((m, LANES), F32)],
        compiler_params=_params(1),
        name="rmsnorm_gates",
    )(x, g.reshape(1, d), wg_hi, wg_lo, bg)


def _mm_kernel(*refs, n_a, n_e, n_o, epilogue, w_transposed, w_resident):
    a_refs = refs[:n_a]
    w_refs = refs[n_a:2 * n_a]
    e_refs = refs[2 * n_a:2 * n_a + n_e]
    o_refs = refs[2 * n_a + n_e:2 * n_a + n_e + n_o]
    if w_resident:
        wb_refs = refs[2 * n_a + n_e + n_o:]

        @pl.when(pl.program_id(1) == 0)
        def _():
            for w_ref, wb_ref in zip(w_refs, wb_refs):
                wb_ref[...] = w_ref[...].astype(BF16)

        w_refs = wb_refs
    acc = None
    for a_ref, w_ref in zip(a_refs, w_refs):
        if w_transposed:
            d = lax.dot_general(a_ref[...], w_ref[...], NT_DIMS, preferred_element_type=F32)
        else:
            d = _dot(a_ref[...], w_ref[...])
        acc = d if acc is None else acc + d
    epilogue(acc, e_refs, o_refs)


def _matmul(name, a_list, w_list, extras, epilogue, out_dtypes, n, tm, tn, w_transposed=False, w_resident=True):
    m = a_list[0].shape[0]
    on_grid = (lambda f: (lambda j, i: f(i, j))) if w_resident else (lambda f: f)
    in_specs, operands, scratch = [], [], []
    for a in a_list:
        in_specs.append(pl.BlockSpec((tm, a.shape[1]), on_grid(lambda i, j: (i, 0))))
        operands.append(a)
    for w, k, rb, cb in w_list:
        if w_transposed:
            blk, imap = (tn, k), (lambda i, j, rb=rb, cb=cb: (j + cb, rb))
        else:
            blk, imap = (k, tn), (lambda i, j, rb=rb, cb=cb: (rb, j + cb))
        in_specs.append(pl.BlockSpec(blk, on_grid(imap)))
        operands.append(w)
        if w_resident:
            scratch.append(pltpu.VMEM(blk, BF16))
        else:
            assert w.dtype == BF16
    for arr, blk, imap in extras:
        in_specs.append(pl.BlockSpec(blk, on_grid(imap)))
        operands.append(arr)
    out_specs = [pl.BlockSpec((tm, tn), on_grid(lambda i, j: (i, j))) for _ in out_dtypes]
    out_shape = [jax.ShapeDtypeStruct((m, n), dt) for dt in out_dtypes]
    return pl.pallas_call(
        functools.partial(_mm_kernel, n_a=len(a_list), n_e=len(extras), n_o=len(out_dtypes), epilogue=epilogue,
                          w_transposed=w_transposed, w_resident=w_resident),
        grid=(n // tn, m // tm) if w_resident else (m // tm, n // tn),
        in_specs=in_specs,
        out_specs=out_specs,
        out_shape=out_shape,
        scratch_shapes=scratch,
        compiler_params=_params(2),
        name=name,
    )(*operands)


def _ep_plain(acc, e_refs, o_refs):
    for o_ref in o_refs:
        o_ref[...] = acc.astype(o_ref.dtype)


def _ep_headnorm(acc, e_refs, o_refs, *, scale):
    g = e_refs[0][...]
    for c in range(acc.shape[1] // HEAD_DIM):
        blk = acc[:, c * HEAD_DIM:(c + 1) * HEAD_DIM]
        y = _rms(blk, g)
        if scale != 1.0:
            y = y * scale
        for o_ref in o_refs:
            o_ref[:, c * HEAD_DIM:(c + 1) * HEAD_DIM] = y.astype(o_ref.dtype)


def _ep_residual(acc, e_refs, o_refs):
    o_refs[0][...] = e_refs[0][...] + acc


def _ep_relu2(acc, e_refs, o_refs):
    r = jnp.maximum(acc, 0.0)
    o_refs[0][...] = (r * r).astype(o_refs[0].dtype)


def _ep_ple(acc, e_refs, o_refs):
    p_ref, wple_ref, h_ref = e_refs
    emb = _dot(p_ref[...].astype(BF16), wple_ref[...])
    o_refs[0][...] = h_ref[...] + emb * _sigmoid(acc)


def _softplus(z):
    return jnp.maximum(z, 0.0) + jnp.log(1.0 + jnp.exp(-jnp.abs(z)))


def _split_trunc(x):
    hi_f = lax.bitcast_convert_type(lax.bitcast_convert_type(x, jnp.uint32) & jnp.uint32(0xFFFF0000), F32)
    return hi_f.astype(BF16), (x - hi_f).astype(BF16)


def _softplus2(zz):
    neg_abs = lax.bitcast_convert_type(lax.bitcast_convert_type(zz, jnp.uint32) | jnp.uint32(0x80000000), F32)
    return jnp.maximum(zz, 0.0) + jnp.log(1.0 + jnp.exp2(neg_abs)) * LOG2E


def _sb_prompt_body(h, i, b_ref, q_ref, k_ref, v_ref, p_ref, o_ref, c_ref, acc_ref, tq):
    bias = b_ref[h]
    q = q_ref[...]
    p2 = p_ref[...]

    def tile(first_key, width, masked):
        start = pl.multiple_of(first_key, tq)
        kt = k_ref[pl.ds(start, width), :]
        vt = v_ref[pl.ds(start, width), :]
        zz = lax.dot_general(q, kt, NT_DIMS, preferred_element_type=F32) + bias
        sp = _softplus2(zz)
        if masked:
            mask = (lax.broadcasted_iota(jnp.int32, (tq, width), 1)
                    < lax.broadcasted_iota(jnp.int32, (tq, width), 0))
            sp = jnp.where(mask, sp, 0.0)
        hi, lo = _split_trunc(sp)
        c = c_ref[...]
        n_sub = width // LANES
        es = [None] * n_sub
        for u in reversed(range(n_sub)):
            sl = slice(u * LANES, (u + 1) * LANES)
            r = _dot(jnp.concatenate([hi[:, sl], lo[:, sl]], axis=1), p2)
            es[u] = jnp.exp2(zz[:, sl] + r[:, :LANES] + c)
            c = c + r[:, LANES:]
        c_ref[...] = c
        a = jnp.concatenate(es, axis=1)
        if masked:
            a = jnp.where(mask, a, 0.0)
        acc_ref[...] += _dot(a.astype(BF16), vt)

    c_ref[...] = jnp.zeros_like(c_ref)
    acc_ref[...] = jnp.zeros_like(acc_ref)
    tile(i * tq, tq, True)
    odd = i % 2

    @pl.when(odd == 1)
    def _():
        tile((i - 1) * tq, tq, False)

    def body(s, carry):
        tile((i - odd - 2 - 2 * s) * tq, 2 * tq, False)
        return carry

    lax.fori_loop(0, i // 2, body, 0)
    o_ref[...] = acc_ref[...].astype(o_ref.dtype)


def _cumsum_matrix(group, inclusive):
    lp = lax.broadcasted_iota(jnp.int32, (2 * LANES, 2 * LANES), 0) % LANES
    l = lax.broadcasted_iota(jnp.int32, (2 * LANES, 2 * LANES), 1)
    same = (lp % group) == (l % group)
    later = ((lp >= l) if inclusive else (lp > l)) | (l >= LANES)
    return jnp.where(same & later, -1.0, 0.0).astype(BF16)


def _sb_sample_body(q_ref, b_ref, p_ref, u_ref, k_refs, v_refs, o_ref, z_ref):
    n_pages = len(k_refs)
    rows = k_refs[0].shape[2] * N_HEADS
    n_tiles = rows // LANES

    q8 = q_ref[0] * QK_SCALE
    zero = jnp.zeros_like(q8)
    q16 = jnp.concatenate([jnp.concatenate([q8, zero], axis=1),
                           jnp.concatenate([zero, q8], axis=1)], axis=0).astype(BF16)
    diag = (lax.broadcasted_iota(jnp.int32, (N_HEADS, rows), 0)
            == lax.broadcasted_iota(jnp.int32, (N_HEADS, rows), 1) % N_HEADS)

    def pick(x):
        return jnp.sum(jnp.where(diag, x, 0.0), axis=0, keepdims=True)

    for pp in range(n_pages // 2):
        ka = k_refs[2 * pp][0, 0].reshape(rows, HEAD_DIM).astype(BF16)
        kb = k_refs[2 * pp + 1][0, 0].reshape(rows, HEAD_DIM).astype(BF16)
        res = lax.dot_general(q16, jnp.concatenate([ka, kb], axis=1), NT_DIMS, preferred_element_type=F32)
        z_ref[2 * pp:2 * pp + 1, :] = pick(res[:N_HEADS])
        z_ref[2 * pp + 1:2 * pp + 2, :] = pick(res[N_HEADS:])

    z = z_ref[...] + b_ref[...]
    sp = _softplus(z)
    hi, lo = _split_trunc(sp)
    p2 = p_ref[...]
    within, carry = [None] * n_tiles, [None] * n_tiles
    run = jnp.zeros((n_pages, LANES), F32)
    for c in reversed(range(n_tiles)):
        sl = slice(c * LANES, (c + 1) * LANES)
        r = _dot(jnp.concatenate([hi[:, sl], lo[:, sl]], axis=1), p2)
        within[c] = r[:, :LANES]
        carry[c] = run
        run = run + r[:, LANES:]
    later_pages = sum(_dot(u_ref[...], x) for x in _split3(run))
    t = z - sp
    a = jnp.concatenate([jnp.exp(t[:, c * LANES:(c + 1) * LANES] + within[c] + carry[c] + later_pages)
                         for c in range(n_tiles)], axis=1)

    acc = jnp.zeros((N_HEADS, HEAD_DIM), F32)
    for pp in range(n_pages // 2):
        sel = jnp.concatenate(
            [jnp.where(diag, jnp.broadcast_to(a[2 * pp:2 * pp + 1], (N_HEADS, rows)), 0.0),
             jnp.where(diag, jnp.broadcast_to(a[2 * pp + 1:2 * pp + 2], (N_HEADS, rows)), 0.0)],
            axis=0).astype(BF16)
        va = v_refs[2 * pp][0, 0].reshape(rows, HEAD_DIM).astype(BF16)
        vb = v_refs[2 * pp + 1][0, 0].reshape(rows, HEAD_DIM).astype(BF16)
        r = _dot(sel, jnp.concatenate([va, vb], axis=1))
        acc = acc + r[:N_HEADS, :HEAD_DIM] + r[N_HEADS:, HEAD_DIM:]
    o_ref[0] = acc.astype(o_ref.dtype)


def _sb_attention_kernel(pt_ref, bp_ref, qp_ref, kp_ref, vp_ref, pp_ref, qs_ref, bs_ref, ps_ref, u_ref, *refs,
                         n_pages, tq):
    del pt_ref
    k_refs = refs[:n_pages]
    v_refs = refs[n_pages:2 * n_pages]
    op_ref, os_ref, c_ref, acc_ref, z_ref = refs[2 * n_pages:]
    _sb_sample_body(qs_ref, bs_ref, ps_ref, u_ref, k_refs, v_refs, os_ref, z_ref)
    _sb_prompt_body(pl.program_id(0), pl.program_id(1), bp_ref, qp_ref, kp_ref, vp_ref, pp_ref, op_ref,
                    c_ref, acc_ref, tq)


def _sb_attention(q, k, v, b_sb, tq, qs, cache_k, cache_v, page_table):
    t = q.shape[0]
    n_blocks = t // tq
    nb, n_pages = page_table.shape
    assert nb == N_HEADS * n_blocks, "one sample sequence per (head, query block) grid step"
    page = cache_k.shape[2]
    rows = page * N_HEADS
    b_row = jnp.tile(b_sb, page).reshape(1, rows)
    pi = lax.broadcasted_iota(jnp.int32, (n_pages, n_pages), 0)
    pj = lax.broadcasted_iota(jnp.int32, (n_pages, n_pages), 1)
    later = (pj > pi).astype(BF16)
    seq = lambda h, i: h * n_blocks + i
    kv_specs = [pl.BlockSpec((1, 1, page, N_HEADS, HEAD_DIM),
                             lambda h, i, pt, j=j: (0, pt[seq(h, i), j], 0, 0, 0))
                for j in range(n_pages)]
    const = lambda shape: pl.BlockSpec(shape, lambda h, i, pt: (0,) * len(shape))
    grid_spec = pltpu.PrefetchScalarGridSpec(
        num_scalar_prefetch=1,
        grid=(N_HEADS, n_blocks),
        in_specs=[pl.BlockSpec(memory_space=pltpu.SMEM),
                  pl.BlockSpec((tq, HEAD_DIM), lambda h, i, pt: (i, h)),
                  pl.BlockSpec((t, HEAD_DIM), lambda h, i, pt: (0, h)),
                  pl.BlockSpec((t, HEAD_DIM), lambda h, i, pt: (0, h)),
                  const((2 * LANES, 2 * LANES)),
                  pl.BlockSpec((1, N_HEADS, HEAD_DIM), lambda h, i, pt: (seq(h, i), 0, 0)),
                  const((1, rows)),
                  const((2 * LANES, 2 * LANES)),
                  const((n_pages, n_pages))] + kv_specs + kv_specs,
        out_specs=[pl.BlockSpec((tq, HEAD_DIM), lambda h, i, pt: (i, h)),
                   pl.BlockSpec((1, N_HEADS, HEAD_DIM), lambda h, i, pt: (seq(h, i), 0, 0))],
        scratch_shapes=[pltpu.VMEM((tq, LANES), F32), pltpu.VMEM((tq, HEAD_DIM), F32),
                        pltpu.VMEM((n_pages, rows), F32)])
    return pl.pallas_call(
        functools.partial(_sb_attention_kernel, n_pages=n_pages, tq=tq),
        grid_spec=grid_spec,
        out_shape=[jax.ShapeDtypeStruct((t, WIDTH), BF16),
                   jax.ShapeDtypeStruct((nb, N_HEADS, HEAD_DIM), BF16)],
        compiler_params=_params(2),
        name="sb_attention",
    )(page_table, b_sb * LOG2E, q, k, v, _cumsum_matrix(1, True),
      qs, b_row, _cumsum_matrix(N_HEADS, False), later, *([cache_k] * n_pages), *([cache_v] * n_pages))


def _ml_prompt_kernel(q_ref, k_ref, v_ref, og_ref, g_ref, gt_ref, gmh_ref, tril_ref, triu_ref, sel_ref,
                      hm_ref, c_ref, n_ref, m_ref, *, chunk):
    step = pl.program_id(0)

    @pl.when(step == 0)
    def _():
        c_ref[...] = jnp.zeros_like(c_ref)
        n_ref[...] = jnp.zeros_like(n_ref)
        m_ref[...] = jnp.zeros_like(m_ref)

    g = g_ref[...]
    gt = gt_ref[...]
    lane = lax.broadcasted_iota(jnp.int32, g.shape, 1)
    rowi = lax.broadcasted_iota(jnp.int32, gt.shape, 0)
    gl = jnp.where(lane >= N_HEADS, _log_sigmoid(g), g)
    gtl = jnp.where(rowi >= N_HEADS, _log_sigmoid(gt), gt)
    tril = tril_ref[...]
    triu = triu_ref[...]
    a_col = sum(_dot(tril, x) for x in _split3(gl))
    a_row = sum(_dot(x, triu) for x in _split3(gtl))
    xs = _split3(jnp.concatenate([gl, a_col], axis=1))
    tt = lax.broadcasted_iota(jnp.int32, (chunk, chunk), 0)
    ss = lax.broadcasted_iota(jnp.int32, (chunk, chunk), 1)
    causal = ss <= tt
    gmh = gmh_ref[...]
    ones = jnp.ones((chunk, HEAD_DIM), BF16)
    ones_sq = jnp.ones((HEAD_DIM, HEAD_DIM), BF16)

    def wide(x):
        return jnp.concatenate([x] * (chunk // LANES), axis=1)

    for h in range(N_HEADS):
        sl = slice(h * HEAD_DIM, (h + 1) * HEAD_DIM)
        lb = sum(_dot(x, sel_ref[h]) for x in xs)
        li_rep = lb[:, :LANES]
        b_rep = lb[:, LANES:]
        b_row = a_row[N_HEADS + h:N_HEADS + h + 1, :]
        li_row = gtl[h:h + 1, :]
        m_old = m_ref[h:h + 1, :]
        b_end = b_rep[chunk - 1:chunk, :]

        dmat = wide(b_rep) - (b_row - li_row)
        inter = b_rep + m_old
        dmax = jnp.max(jnp.where(causal, dmat, -jnp.inf), axis=-1, keepdims=True)
        m_row = jnp.maximum(inter, dmax)
        e = jnp.where(causal, jnp.exp(dmat - wide(m_row)), 0.0)
        w_inter = jnp.exp(inter - m_row)

        qc = q_ref[:, sl]
        kc = k_ref[:, sl] * QK_SCALE
        vc = v_ref[:, sl]
        qb = qc.astype(BF16)
        kb = kc.astype(BF16)
        s = lax.dot_general(qb, kb, NT_DIMS, preferred_element_type=F32) * e
        c_old = c_ref[h]
        n_old = n_ref[h:h + 1, :]
        r1 = _dot(s.astype(BF16), jnp.concatenate([vc.astype(BF16), ones], axis=1))
        cn = jnp.concatenate([c_old, jnp.broadcast_to(n_old, (HEAD_DIM, HEAD_DIM))], axis=0).astype(BF16)
        r2 = lax.dot_general(qb, cn, NT_DIMS, preferred_element_type=F32)
        num = r1[:, :HEAD_DIM] + w_inter * r2[:, :HEAD_DIM]
        den = r1[:, HEAD_DIM:] + w_inter * r2[:, HEAD_DIM:]
        hh = num / jnp.maximum(jnp.abs(den), jnp.exp(-m_row))
        sq_hi, sq_lo = _split2(hh * hh)
        ms = (_dot(sq_hi, ones_sq) + _dot(sq_lo, ones_sq)) * (1.0 / HEAD_DIM)
        hn = hh * lax.rsqrt(ms + RMS_EPS) * gmh
        hm_ref[:, sl] = (hn * _sigmoid(og_ref[:, sl])).astype(hm_ref.dtype)

        gw = b_end - b_rep + li_rep
        m_new = jnp.maximum(b_end + m_old, jnp.max(gw, axis=0, keepdims=True))
        wk = jnp.exp(gw - m_new)
        decay = jnp.exp(b_end + m_old - m_new)
        c_ref[h] = decay * c_old + lax.dot_general((wk * vc).astype(BF16), kb, TN_DIMS,
                                                   preferred_element_type=F32)
        n_ref[h:h + 1, :] = decay * n_old + jnp.sum(wk * kc, axis=0, keepdims=True)
        m_ref[h:h + 1, :] = m_new


def _gate_select_matrices():
    shape = (N_HEADS, 2 * LANES, 2 * LANES)
    h = lax.broadcasted_iota(jnp.int32, shape, 0)
    r = lax.broadcasted_iota(jnp.int32, shape, 1)
    c = lax.broadcasted_iota(jnp.int32, shape, 2)
    return jnp.where(c < LANES, r == h, r == LANES + N_HEADS + h).astype(BF16)


def _ml_prompt(ml, gates, gates_t, g_mh, chunk):
    t = ml.shape[0]
    tri = jnp.tril(jnp.ones((chunk, chunk), F32)).astype(BF16)
    col = lambda c: pl.BlockSpec((chunk, WIDTH), lambda s, c=c: (s, c))
    return pl.pallas_call(
        functools.partial(_ml_prompt_kernel, chunk=chunk),
        grid=(t // chunk,),
        in_specs=[col(0), col(1), col(2), col(3),
                  pl.BlockSpec((chunk, LANES), lambda s: (s, 0)),
                  pl.BlockSpec((2 * N_HEADS, chunk), lambda s: (0, s)),
                  pl.BlockSpec((1, HEAD_DIM), lambda s: (0, 0)),
                  pl.BlockSpec((chunk, chunk), lambda s: (0, 0)),
                  pl.BlockSpec((chunk, chunk), lambda s: (0, 0)),
                  pl.BlockSpec((N_HEADS, 2 * LANES, 2 * LANES), lambda s: (0, 0, 0))],
        out_specs=[pl.BlockSpec((chunk, WIDTH), lambda s: (s, 0)),
                   pl.BlockSpec((N_HEADS, HEAD_DIM, HEAD_DIM), lambda s: (0, 0, 0)),
                   pl.BlockSpec((N_HEADS, HEAD_DIM), lambda s: (0, 0)),
                   pl.BlockSpec((N_HEADS, LANES), lambda s: (0, 0))],
        out_shape=[jax.ShapeDtypeStruct((t, WIDTH), BF16),
                   jax.ShapeDtypeStruct((N_HEADS, HEAD_DIM, HEAD_DIM), F32),
                   jax.ShapeDtypeStruct((N_HEADS, HEAD_DIM), F32),
                   jax.ShapeDtypeStruct((N_HEADS, LANES), F32)],
        compiler_params=_params(1),
        name="mlstm_prompt",
    )(ml, ml, ml, ml, gates, gates_t, g_mh.reshape(1, HEAD_DIM), tri, tri.T, _gate_select_matrices())


def _ml_sample_kernel(x_ref, g_ref, m_ref, c_ref, n_ref, gmh_ref, hm_ref, cn_ref, nn_ref, mn_ref):
    for b in range(x_ref.shape[0]):
        _ml_sample_one(b, x_ref, g_ref, m_ref, c_ref, n_ref, gmh_ref, hm_ref, cn_ref, nn_ref, mn_ref)


def _ml_sample_one(b, x_ref, g_ref, m_ref, c_ref, n_ref, gmh_ref, hm_ref, cn_ref, nn_ref, mn_ref):
    x = x_ref[b]
    q = x[0:N_HEADS]
    k = x[N_HEADS:2 * N_HEADS] * QK_SCALE
    v = x[2 * N_HEADS:3 * N_HEADS]
    og = x[3 * N_HEADS:4 * N_HEADS]
    gates = g_ref[b]
    li = gates[0:N_HEADS]
    lf = _log_sigmoid(gates[N_HEADS:2 * N_HEADS])
    m_old = m_ref[b]
    n_old = n_ref[b]

    inter = lf + m_old
    m_new = jnp.maximum(inter, li)
    s = jnp.sum(q * k, axis=-1, keepdims=True) * jnp.exp(li - m_new)
    w_inter = jnp.exp(inter - m_new)
    qb = q.astype(BF16)
    rowh = lax.broadcasted_iota(jnp.int32, (N_HEADS, HEAD_DIM), 0)
    cq = jnp.zeros((N_HEADS, HEAD_DIM), F32)
    for h in range(N_HEADS):
        r = lax.dot_general(qb, c_ref[b, h].astype(BF16), NT_DIMS, preferred_element_type=F32)
        cq = jnp.where(rowh == h, r, cq)
    num = s * v + w_inter * cq
    den = s + w_inter * jnp.sum(n_old * q, axis=-1, keepdims=True)
    hh = num / jnp.maximum(jnp.abs(den), jnp.exp(-m_new))
    hm_ref[b] = (_rms(hh, gmh_ref[...]) * _sigmoid(og)).astype(hm_ref.dtype)

    wk = jnp.exp(li - m_new)
    decay = w_inter
    nn_ref[b] = decay * n_old + wk * k
    mn_ref[b] = m_new
    wv = wk * v
    wvt = jnp.concatenate([wv, jnp.zeros((LANES - N_HEADS, HEAD_DIM), F32)], axis=0).T
    for h in range(N_HEADS):
        cn_ref[b, h] = decay[h:h + 1, :] * c_ref[b, h] + wvt[:, h:h + 1] * k[h:h + 1, :]


def _ml_sample(ml, gates, state_m, state_c, state_n, g_mh, group):
    nb = ml.shape[0]
    x = ml.reshape(nb, 4 * N_HEADS, HEAD_DIM)
    g = gates.reshape(nb, 2 * N_HEADS, 1)
    m = state_m.reshape(nb, N_HEADS, 1)
    b3 = lambda *shape: pl.BlockSpec((group,) + shape, lambda b: (b,) + (0,) * len(shape))
    return pl.pallas_call(
        _ml_sample_kernel,
        grid=(nb // group,),
        in_specs=[b3(4 * N_HEADS, HEAD_DIM), b3(2 * N_HEADS, 1), b3(N_HEADS, 1),
                  b3(N_HEADS, HEAD_DIM, HEAD_DIM), b3(N_HEADS, HEAD_DIM),
                  pl.BlockSpec((1, HEAD_DIM), lambda b: (0, 0))],
        out_specs=[b3(N_HEADS, HEAD_DIM), b3(N_HEADS, HEAD_DIM, HEAD_DIM), b3(N_HEADS, HEAD_DIM), b3(N_HEADS, 1)],
        out_shape=[jax.ShapeDtypeStruct((nb, N_HEADS, HEAD_DIM), BF16),
                   jax.ShapeDtypeStruct((nb, N_HEADS, HEAD_DIM, HEAD_DIM), F32),
                   jax.ShapeDtypeStruct((nb, N_HEADS, HEAD_DIM), F32),
                   jax.ShapeDtypeStruct((nb, N_HEADS, 1), F32)],
        compiler_params=_params(1),
        name="mlstm_sample",
    )(x, g, m, state_c, state_n, g_mh.reshape(1, HEAD_DIM))


def _dense_front(x, w, tm, q_scale, q_dtype, kv_dtypes):
    xn, gates = _norm_gate(x, w["g_mix"], w["wg_hi"], w["wg_lo"], w["bg"], min(tm, 512))
    d = x.shape[1]
    gq = (w["g_q"].reshape(1, HEAD_DIM), (1, HEAD_DIM), lambda i, j: (0, 0))
    gk = (w["g_k"].reshape(1, HEAD_DIM), (1, HEAD_DIM), lambda i, j: (0, 0))
    tn = 1024
    w_in_t = w["w_in_t"]
    (q,) = _matmul("proj_q", [xn], [(w_in_t, d, 0, 0)], [gq],
                   functools.partial(_ep_headnorm, scale=q_scale), [q_dtype], WIDTH, tm, tn, True)
    k = _matmul("proj_k", [xn], [(w_in_t, d, 0, WIDTH // tn)], [gk],
                functools.partial(_ep_headnorm, scale=1.0), kv_dtypes, WIDTH, tm, tn, True)
    v = _matmul("proj_v", [xn], [(w_in_t, d, 0, 2 * WIDTH // tn)], [], _ep_plain, kv_dtypes, WIDTH, tm, tn, True)
    (ml,) = _matmul("proj_ml", [xn], [(w_in_t, d, 0, 3 * WIDTH // tn)], [], _ep_plain, [F32], 4 * WIDTH, tm, tn,
                    True)
    return q, k, v, ml, gates


def _dense_back(x, sb, hm, p, w, tm):
    m, d = x.shape
    tn = 512
    res = lambda arr: (arr, (tm, tn), lambda i, j: (i, j))
    (h1,) = _matmul("out_proj", [sb, hm], [(w["w_out"], WIDTH, 0, 0), (w["w_out"], WIDTH, 1, 0)], [res(x)],
                    _ep_residual, [F32], d, tm, tn)
    hn = _norm(h1, w["g_ffn"], min(tm, 512))
    d_ff = w["w_up"].shape[1]
    (act,) = _matmul("ffn_up", [hn], [(w["w_up"], d, 0, 0)], [], _ep_relu2, [BF16], d_ff, tm, 1024)
    tm2, tn2 = tm, 256
    res2 = lambda arr: (arr, (tm2, tn2), lambda i, j: (i, j))
    (h2,) = _matmul("ffn_down", [act], [(w["w_down_bf16"], d_ff, 0, 0)], [res2(h1)], _ep_residual, [F32], d,
                    tm2, tn2, w_resident=False)
    hn2 = _norm(h2, w["g_ple"], min(tm, 512))
    ple = p.shape[1]
    extras = [(p, (tm, ple), lambda i, j: (i, 0)),
              (w["w_ple_bf16"], (ple, tn), lambda i, j: (0, j)),
              res(h2)]
    (y,) = _matmul("ple_gate", [hn2], [(w["w_pg"], d, 0, 0)], extras, _ep_ple, [F32], d, tm, tn)
    return y


def kernel(x_prompt, x_sample, cache_k, cache_v, state_C, state_n, state_m, page_table, p_prompt, p_sample,
           g_mix, w_in, b_gate, b_sb, g_q, g_k, g_mh, w_out, g_ffn, w_up, w_down, g_ple, w_ple, w_pg):
    depth = w_in.shape[0]
    assert depth == 1 and x_prompt.shape[0] == 1 and x_sample.shape[1] == 1
    t, d = x_prompt.shape[1], x_prompt.shape[2]
    nb = x_sample.shape[0]
    n_proj = w_in.shape[2] - 2 * N_HEADS

    wg = jnp.pad(w_in[0, :, n_proj:], ((0, 0), (0, LANES - 2 * N_HEADS)))
    wg_hi = wg.astype(BF16)
    w = {
        "g_mix": g_mix[0], "g_q": g_q[0], "g_k": g_k[0], "g_ffn": g_ffn[0], "g_ple": g_ple[0],
        "wg_hi": wg_hi, "wg_lo": (wg - wg_hi.astype(F32)).astype(BF16),
        "bg": jnp.pad(b_gate[0], (0, LANES - 2 * N_HEADS)).reshape(1, LANES),
        "w_in_t": jnp.swapaxes(w_in[0], 0, 1), "w_out": w_out[0], "w_up": w_up[0], "w_pg": w_pg[0],
        "w_down_bf16": w_down[0].astype(BF16), "w_ple_bf16": w_ple[0].astype(BF16),
    }

    xp = x_prompt[0]
    xs = x_sample[:, 0]
    q, (k, k_bf), (v, v_bf), ml, gates = _dense_front(xp, w, 1024, QK_SCALE * LOG2E, BF16, [F32, BF16])
    qs, (ks,), (vs,), mls, gates_s = _dense_front(xs, w, nb, 1.0, F32, [F32])
    sb, sbs = _sb_attention(q, k_bf, v_bf, b_sb[0], 512, qs.reshape(nb, N_HEADS, HEAD_DIM),
                            cache_k, cache_v, page_table)
    chunk = 256
    gates_t = gates[:, :2 * N_HEADS].T
    hm, c_p, n_p, m_p = _ml_prompt(ml, gates, gates_t, g_mh[0], chunk)
    y_p = _dense_back(xp, sb, hm, p_prompt[0, 0], w, 1024)
    hms, c_s, n_s, m_s = _ml_sample(mls, gates_s[:, :2 * N_HEADS], state_m[0], state_C[0], state_n[0], g_mh[0], 8)
    y_s = _dense_back(xs, sbs.reshape(nb, WIDTH), hms.reshape(nb, WIDTH), p_sample[0, :, 0], w, nb)

    hd = (N_HEADS, HEAD_DIM)
    return (y_p[None], y_s[:, None],
            k.reshape((1, 1, t) + hd), v.reshape((1, 1, t) + hd),
            c_p[None, None], n_p[None, None], m_p[:, 0][None, None],
            ks.reshape((1, nb, 1) + hd), vs.reshape((1, nb, 1) + hd),
            c_s[None], n_s[None], m_s.reshape(1, nb, N_HEADS))
```

```python
import functools
import math

import jax
import jax.numpy as jnp
from jax import lax
from jax.experimental import pallas as pl
from jax.experimental.pallas import tpu as pltpu

F32 = jnp.float32
BF16 = jnp.bfloat16

HEAD_DIM = 128
N_HEADS = 8
WIDTH = N_HEADS * HEAD_DIM
RMS_EPS = 1e-6
QK_SCALE = HEAD_DIM ** -0.5
LOG2E = 1.0 / math.log(2.0)
LANES = 128
VMEM_LIMIT = 56 * 1024 * 1024

NT_DIMS = (((1,), (1,)), ((), ()))
TN_DIMS = (((0,), (0,)), ((), ()))


def _params(n_grid_dims):
    return pltpu.CompilerParams(
        dimension_semantics=("arbitrary",) * n_grid_dims,
        vmem_limit_bytes=VMEM_LIMIT)


def _log_sigmoid(x):
    return jnp.minimum(x, 0.0) - jnp.log1p(jnp.exp(-jnp.abs(x)))


def _sigmoid(x):
    return 1.0 / (1.0 + jnp.exp(-x))


def _split2(x):
    hi = x.astype(BF16)
    lo = (x - hi.astype(F32)).astype(BF16)
    return hi, lo


def _split3(x):
    x1 = x.astype(BF16)
    r1 = x - x1.astype(F32)
    x2 = r1.astype(BF16)
    x3 = (r1 - x2.astype(F32)).astype(BF16)
    return x1, x2, x3


def _dot(a, b):
    return jnp.dot(a, b, preferred_element_type=F32)


def _rms(x, g):
    return x * lax.rsqrt(jnp.mean(x * x, axis=-1, keepdims=True) + RMS_EPS) * g


def _norm_kernel(x_ref, g_ref, o_ref):
    o_ref[...] = _rms(x_ref[...], g_ref[...]).astype(o_ref.dtype)


def _norm_gate_kernel(x_ref, g_ref, wh_ref, wl_ref, b_ref, o_ref, gate_ref):
    y = _rms(x_ref[...], g_ref[...])
    yh, yl = _split2(y)
    o_ref[...] = yh
    wh = wh_ref[...]
    gate_ref[...] = _dot(yh, wh) + _dot(yh, wl_ref[...]) + _dot(yl, wh) + b_ref[...]


def _norm(x, g, tm):
    m, d = x.shape
    return pl.pallas_call(
        _norm_kernel,
        grid=(m // tm,),
        in_specs=[pl.BlockSpec((tm, d), lambda i: (i, 0)),
                  pl.BlockSpec((1, d), lambda i: (0, 0))],
        out_specs=pl.BlockSpec((tm, d), lambda i: (i, 0)),
        out_shape=jax.ShapeDtypeStruct((m, d), BF16),
        compiler_params=_params(1),
        name="rmsnorm",
    )(x, g.reshape(1, d))


def _norm_gate(x, g, wg_hi, wg_lo, bg, tm):
    m, d = x.shape
    return pl.pallas_call(
        _norm_gate_kernel,
        grid=(m // tm,),
        in_specs=[pl.BlockSpec((tm, d), lambda i: (i, 0)),
                  pl.BlockSpec((1, d), lambda i: (0, 0)),
                  pl.BlockSpec((d, LANES), lambda i: (0, 0)),
                  pl.BlockSpec((d, LANES), lambda i: (0, 0)),
                  pl.BlockSpec((1, LANES), lambda i: (0, 0))],
        out_specs=[pl.BlockSpec((tm, d), lambda i: (i, 0)),
                   pl.BlockSpec((tm, LANES), lambda i: (i, 0))],
        out_shape=[jax.ShapeDtypeStruct((m, d), BF16),
                   jax.ShapeDtypeStruct((m, LANES), F32)],
        compiler_params=_params(1),
        name="rmsnorm_gates",
    )(x, g.reshape(1, d), wg_hi, wg_lo, bg)


def _mm_kernel(*refs, n_a, n_e, n_o, epilogue, w_transposed, w_resident):
    a_refs = refs[:n_a]
    w_refs = refs[n_a:2 * n_a]
    e_refs = refs[2 * n_a:2 * n_a + n_e]
    o_refs = refs[2 * n_a + n_e:2 * n_a + n_e + n_o]
    if w_resident:
        wb_refs = refs[2 * n_a + n_e + n_o:]

        @pl.when(pl.program_id(1) == 0)
        def _():
            for w_ref, wb_ref in zip(w_refs, wb_refs):
                wb_ref[...] = w_ref[...].astype(BF16)

        w_refs = wb_refs
    acc = None
    for a_ref, w_ref in zip(a_refs, w_refs):
        if w_transposed:
            d = lax.dot_general(a_ref[...], w_ref[...], NT_DIMS, preferred_element_type=F32)
        else:
            d = _dot(a_ref[...], w_ref[...])
        acc = d if acc is None else acc + d
    epilogue(acc, e_refs, o_refs)


def _matmul(name, a_list, w_list, extras, epilogue, out_dtypes, n, tm, tn, w_transposed=False, w_resident=True):
    m = a_list[0].shape[0]
    on_grid = (lambda f: (lambda j, i: f(i, j))) if w_resident else (lambda f: f)
    in_specs, operands, scratch = [], [], []
    for a in a_list:
        in_specs.append(pl.BlockSpec((tm, a.shape[1]), on_grid(lambda i, j: (i, 0))))
        operands.append(a)
    for w, k, rb, cb in w_list:
        if w_transposed:
            blk, imap = (tn, k), (lambda i, j, rb=rb, cb=cb: (j + cb, rb))
        else:
            blk, imap = (k, tn), (lambda i, j, rb=rb, cb=cb: (rb, j + cb))
        in_specs.append(pl.BlockSpec(blk, on_grid(imap)))
        operands.append(w)
        if w_resident:
            scratch.append(pltpu.VMEM(blk, BF16))
        else:
            assert w.dtype == BF16
    for arr, blk, imap in extras:
        in_specs.append(pl.BlockSpec(blk, on_grid(imap)))
        operands.append(arr)
    out_specs = [pl.BlockSpec((tm, tn), on_grid(lambda i, j: (i, j))) for _ in out_dtypes]
    out_shape = [jax.ShapeDtypeStruct((m, n), dt) for dt in out_dtypes]
    return pl.pallas_call(
        functools.partial(_mm_kernel, n_a=len(a_list), n_e=len(extras), n_o=len(out_dtypes), epilogue=epilogue,
                          w_transposed=w_transposed, w_resident=w_resident),
        grid=(n // tn, m // tm) if w_resident else (m // tm, n // tn),
        in_specs=in_specs,
        out_specs=out_specs,
        out_shape=out_shape,
        scratch_shapes=scratch,
        compiler_params=_params(2),
        name=name,
    )(*operands)


def _ep_plain(acc, e_refs, o_refs):
    for o_ref in o_refs:
        o_ref[...] = acc.astype(o_ref.dtype)


def _ep_headnorm(acc, e_refs, o_refs, *, scale):
    g = e_refs[0][...]
    for c in range(acc.shape[1] // HEAD_DIM):
        blk = acc[:, c * HEAD_DIM:(c + 1) * HEAD_DIM]
        y = _rms(blk, g)
        if scale != 1.0:
            y = y * scale
        for o_ref in o_refs:
            o_ref[:, c * HEAD_DIM:(c + 1) * HEAD_DIM] = y.astype(o_ref.dtype)


def _ep_residual(acc, e_refs, o_refs):
    o_refs[0][...] = e_refs[0][...] + acc


def _ep_relu2(acc, e_refs, o_refs):
    r = jnp.maximum(acc, 0.0)
    o_refs[0][...] = (r * r).astype(o_refs[0].dtype)


def _ep_ple(acc, e_refs, o_refs):
    p_ref, wple_ref, h_ref = e_refs
    emb = _dot(p_ref[...].astype(BF16), wple_ref[...])
    o_refs[0][...] = h_ref[...] + emb * _sigmoid(acc)


def _softplus(z):
    return jnp.maximum(z, 0.0) + jnp.log(1.0 + jnp.exp(-jnp.abs(z)))


def _split_trunc(x):
    hi_f = lax.bitcast_convert_type(lax.bitcast_convert_type(x, jnp.uint32) & jnp.uint32(0xFFFF0000), F32)
    return hi_f.astype(BF16), (x - hi_f).astype(BF16)


def _softplus2(zz):
    neg_abs = lax.bitcast_convert_type(lax.bitcast_convert_type(zz, jnp.uint32) | jnp.uint32(0x80000000), F32)
    return jnp.maximum(zz, 0.0) + jnp.log(1.0 + jnp.exp2(neg_abs)) * LOG2E


def _sb_prompt_body(h, i, b_ref, q_ref, k_ref, v_ref, p_ref, o_ref, c_ref, acc_ref, tq):
    bias = b_ref[h]
    q = q_ref[...]
    p2 = p_ref[...]

    def tile(first_key, width, masked):
        start = pl.multiple_of(first_key, tq)
        kt = k_ref[pl.ds(start, width), :]
        vt = v_ref[pl.ds(start, width), :]
        zz = lax.dot_general(q, kt, NT_DIMS, preferred_element_type=F32) + bias
        sp = _softplus2(zz)
        if masked:
            mask = (lax.broadcasted_iota(jnp.int32, (tq, width), 1)
                    < lax.broadcasted_iota(jnp.int32, (tq, width), 0))
            sp = jnp.where(mask, sp, 0.0)
        hi, lo = _split_trunc(sp)
        c = c_ref[...]
        n_sub = width // LANES
        es = [None] * n_sub
        for u in reversed(range(n_sub)):
            sl = slice(u * LANES, (u + 1) * LANES)
            r = _dot(jnp.concatenate([hi[:, sl], lo[:, sl]], axis=1), p2)
            es[u] = jnp.exp2(zz[:, sl] + r[:, :LANES] + c)
            c = c + r[:, LANES:]
        c_ref[...] = c
        a = jnp.concatenate(es, axis=1)
        if masked:
            a = jnp.where(mask, a, 0.0)
        acc_ref[...] += _dot(a.astype(BF16), vt)

    c_ref[...] = jnp.zeros_like(c_ref)
    acc_ref[...] = jnp.zeros_like(acc_ref)
    tile(i * tq, tq, True)
    odd = i % 2

    @pl.when(odd == 1)
    def _():
        tile((i - 1) * tq, tq, False)

    def body(s, carry):
        tile((i - odd - 2 - 2 * s) * tq, 2 * tq, False)
        return carry

    lax.fori_loop(0, i // 2, body, 0)
    o_ref[...] = acc_ref[...].astype(o_ref.dtype)


def _cumsum_matrix(group, inclusive):
    lp = lax.broadcasted_iota(jnp.int32, (2 * LANES, 2 * LANES), 0) % LANES
    l = lax.broadcasted_iota(jnp.int32, (2 * LANES, 2 * LANES), 1)
    same = (lp % group) == (l % group)
    later = ((lp >= l) if inclusive else (lp > l)) | (l >= LANES)
    return jnp.where(same & later, -1.0, 0.0).astype(BF16)


def _sb_sample_body(q_ref, b_ref, p_ref, u_ref, k_refs, v_refs, o_ref, z_ref):
    n_pages = len(k_refs)
    rows = k_refs[0].shape[2] * N_HEADS
    n_tiles = rows // LANES

    q8 = q_ref[0] * QK_SCALE
    zero = jnp.zeros_like(q8)
    q16 = jnp.concatenate([jnp.concatenate([q8, zero], axis=1),
                           jnp.concatenate([zero, q8], axis=1)], axis=0).astype(BF16)
    diag = (lax.broadcasted_iota(jnp.int32, (N_HEADS, rows), 0)
            == lax.broadcasted_iota(jnp.int32, (N_HEADS, rows), 1) % N_HEADS)

    def pick(x):
        return jnp.sum(jnp.where(diag, x, 0.0), axis=0, keepdims=True)

    for pp in range(n_pages // 2):
        ka = k_refs[2 * pp][0, 0].reshape(rows, HEAD_DIM).astype(BF16)
        kb = k_refs[2 * pp + 1][0, 0].reshape(rows, HEAD_DIM).astype(BF16)
        res = lax.dot_general(q16, jnp.concatenate([ka, kb], axis=1), NT_DIMS, preferred_element_type=F32)
        z_ref[2 * pp:2 * pp + 1, :] = pick(res[:N_HEADS])
        z_ref[2 * pp + 1:2 * pp + 2, :] = pick(res[N_HEADS:])

    z = z_ref[...] + b_ref[...]
    sp = _softplus(z)
    hi, lo = _split_trunc(sp)
    p2 = p_ref[...]
    within, carry = [None] * n_tiles, [None] * n_tiles
    run = jnp.zeros((n_pages, LANES), F32)
    for c in reversed(range(n_tiles)):
        sl = slice(c * LANES, (c + 1) * LANES)
        r = _dot(jnp.concatenate([hi[:, sl], lo[:, sl]], axis=1), p2)
        within[c] = r[:, :LANES]
        carry[c] = run
        run = run + r[:, LANES:]
    later_pages = sum(_dot(u_ref[...], x) for x in _split3(run))
    t = z - sp
    a = jnp.concatenate([jnp.exp(t[:, c * LANES:(c + 1) * LANES] + within[c] + carry[c] + later_pages)
                         for c in range(n_tiles)], axis=1)

    acc = jnp.zeros((N_HEADS, HEAD_DIM), F32)
    for pp in range(n_pages // 2):
        sel = jnp.concatenate(
            [jnp.where(diag, jnp.broadcast_to(a[2 * pp:2 * pp + 1], (N_HEADS, rows)), 0.0),
             jnp.where(diag, jnp.broadcast_to(a[2 * pp + 1:2 * pp + 2], (N_HEADS, rows)), 0.0)],
            axis=0).astype(BF16)
        va = v_refs[2 * pp][0, 0].reshape(rows, HEAD_DIM).astype(BF16)
        vb = v_refs[2 * pp + 1][0, 0].reshape(rows, HEAD_DIM).astype(BF16)
        r = _dot(sel, jnp.concatenate([va, vb], axis=1))
        acc = acc + r[:N_HEADS, :HEAD_DIM] + r[N_HEADS:, HEAD_DIM:]
    o_ref[0] = acc.astype(o_ref.dtype)


def _sb_attention_kernel(pt_ref, bp_ref, qp_ref, kp_ref, vp_ref, pp_ref, qs_ref, bs_ref, ps_ref, u_ref, *refs,
                         n_pages, tq):
    del pt_ref
    k_refs = refs[:n_pages]
    v_refs = refs[n_pages:2 * n_pages]
    op_ref, os_ref, c_ref, acc_ref, z_ref = refs[2 * n_pages:]
    _sb_sample_body(qs_ref, bs_ref, ps_ref, u_ref, k_refs, v_refs, os_ref, z_ref)
    _sb_prompt_body(pl.program_id(0), pl.program_id(1), bp_ref, qp_ref, kp_ref, vp_ref, pp_ref, op_ref,
                    c_ref, acc_ref, tq)


def _sb_attention(q, k, v, b_sb, tq, qs, cache_k, cache_v, page_table):
    t = q.shape[0]
    n_blocks = t // tq
    nb, n_pages = page_table.shape
    assert nb == N_HEADS * n_blocks, "one sample sequence per (head, query block) grid step"
    page = cache_k.shape[2]
    rows = page * N_HEADS
    b_row = jnp.tile(b_sb, page).reshape(1, rows)
    pi = lax.broadcasted_iota(jnp.int32, (n_pages, n_pages), 0)
    pj = lax.broadcasted_iota(jnp.int32, (n_pages, n_pages), 1)
    later = (pj > pi).astype(BF16)
    seq = lambda h, i: h * n_blocks + i
    kv_specs = [pl.BlockSpec((1, 1, page, N_HEADS, HEAD_DIM),
                             lambda h, i, pt, j=j: (0, pt[seq(h, i), j], 0, 0, 0))
                for j in range(n_pages)]
    const = lambda shape: pl.BlockSpec(shape, lambda h, i, pt: (0,) * len(shape))
    grid_spec = pltpu.PrefetchScalarGridSpec(
        num_scalar_prefetch=1,
        grid=(N_HEADS, n_blocks),
        in_specs=[pl.BlockSpec(memory_space=pltpu.SMEM),
                  pl.BlockSpec((tq, HEAD_DIM), lambda h, i, pt: (i, h)),
                  pl.BlockSpec((t, HEAD_DIM), lambda h, i, pt: (0, h)),
                  pl.BlockSpec((t, HEAD_DIM), lambda h, i, pt: (0, h)),
                  const((2 * LANES, 2 * LANES)),
                  pl.BlockSpec((1, N_HEADS, HEAD_DIM), lambda h, i, pt: (seq(h, i), 0, 0)),
                  const((1, rows)),
                  const((2 * LANES, 2 * LANES)),
                  const((n_pages, n_pages))] + kv_specs + kv_specs,
        out_specs=[pl.BlockSpec((tq, HEAD_DIM), lambda h, i, pt: (i, h)),
                   pl.BlockSpec((1, N_HEADS, HEAD_DIM), lambda h, i, pt: (seq(h, i), 0, 0))],
        scratch_shapes=[pltpu.VMEM((tq, LANES), F32), pltpu.VMEM((tq, HEAD_DIM), F32),
                        pltpu.VMEM((n_pages, rows), F32)])
    return pl.pallas_call(
        functools.partial(_sb_attention_kernel, n_pages=n_pages, tq=tq),
        grid_spec=grid_spec,
        out_shape=[jax.ShapeDtypeStruct((t, WIDTH), BF16),
                   jax.ShapeDtypeStruct((nb, N_HEADS, HEAD_DIM), BF16)],
        compiler_params=_params(2),
        name="sb_attention",
    )(page_table, b_sb * LOG2E, q, k, v, _cumsum_matrix(1, True),
      qs, b_row, _cumsum_matrix(N_HEADS, False), later, *([cache_k] * n_pages), *([cache_v] * n_pages))


def _ml_prompt_kernel(q_ref, k_ref, v_ref, og_ref, g_ref, gt_ref, gmh_ref, tril_ref, triu_ref, sel_ref,
                      hm_ref, c_ref, n_ref, m_ref, *, chunk):
    step = pl.program_id(0)

    @pl.when(step == 0)
    def _():
        c_ref[...] = jnp.zeros_like(c_ref)
        n_ref[...] = jnp.zeros_like(n_ref)
        m_ref[...] = jnp.zeros_like(m_ref)

    g = g_ref[...]
    gt = gt_ref[...]
    lane = lax.broadcasted_iota(jnp.int32, g.shape, 1)
    rowi = lax.broadcasted_iota(jnp.int32, gt.shape, 0)
    gl = jnp.where(lane >= N_HEADS, _log_sigmoid(g), g)
    gtl = jnp.where(rowi >= N_HEADS, _log_sigmoid(gt), gt)
    tril = tril_ref[...]
    triu = triu_ref[...]
    a_col = sum(_dot(tril, x) for x in _split3(gl))
    a_row = sum(_dot(x, triu) for x in _split3(gtl))
    xs = _split3(jnp.concatenate([gl, a_col], axis=1))
    tt = lax.broadcasted_iota(jnp.int32, (chunk, chunk), 0)
    ss = lax.broadcasted_iota(jnp.int32, (chunk, chunk), 1)
    causal = ss <= tt
    gmh = gmh_ref[...]
    ones = jnp.ones((chunk, HEAD_DIM), BF16)
    ones_sq = jnp.ones((HEAD_DIM, HEAD_DIM), BF16)

    def wide(x):
        return jnp.concatenate([x] * (chunk // LANES), axis=1)

    for h in range(N_HEADS):
        sl = slice(h * HEAD_DIM, (h + 1) * HEAD_DIM)
        lb = sum(_dot(x, sel_ref[h]) for x in xs)
        li_rep = lb[:, :LANES]
        b_rep = lb[:, LANES:]
        b_row = a_row[N_HEADS + h:N_HEADS + h + 1, :]
        li_row = gtl[h:h + 1, :]
        m_old = m_ref[h:h + 1, :]
        b_end = b_rep[chunk - 1:chunk, :]

        dmat = wide(b_rep) - (b_row - li_row)
        inter = b_rep + m_old
        dmax = jnp.max(jnp.where(causal, dmat, -jnp.inf), axis=-1, keepdims=True)
        m_row = jnp.maximum(inter, dmax)
        e = jnp.where(causal, jnp.exp(dmat - wide(m_row)), 0.0)
        w_inter = jnp.exp(inter - m_row)

        qc = q_ref[:, sl]
        kc = k_ref[:, sl] * QK_SCALE
        vc = v_ref[:, sl]
        qb = qc.astype(BF16)
        kb = kc.astype(BF16)
        s = lax.dot_general(qb, kb, NT_DIMS, preferred_element_type=F32) * e
        c_old = c_ref[h]
        n_old = n_ref[h:h + 1, :]
        r1 = _dot(s.astype(BF16), jnp.concatenate([vc.astype(BF16), ones], axis=1))
        cn = jnp.concatenate([c_old, jnp.broadcast_to(n_old, (HEAD_DIM, HEAD_DIM))], axis=0).astype(BF16)
        r2 = lax.dot_general(qb, cn, NT_DIMS, preferred_element_type=F32)
        num = r1[:, :HEAD_DIM] + w_inter * r2[:, :HEAD_DIM]
        den = r1[:, HEAD_DIM:] + w_inter * r2[:, HEAD_DIM:]
        hh = num / jnp.maximum(jnp.abs(den), jnp.exp(-m_row))
        sq_hi, sq_lo = _split2(hh * hh)
        ms = (_dot(sq_hi, ones_sq) + _dot(sq_lo, ones_sq)) * (1.0 / HEAD_DIM)
        hn = hh * lax.rsqrt(ms + RMS_EPS) * gmh
        hm_ref[:, sl] = (hn * _sigmoid(og_ref[:, sl])).astype(hm_ref.dtype)

        gw = b_end - b_rep + li_rep
        m_new = jnp.maximum(b_end + m_old, jnp.max(gw, axis=0, keepdims=True))
        wk = jnp.exp(gw - m_new)
        decay = jnp.exp(b_end + m_old - m_new)
        c_ref[h] = decay * c_old + lax.dot_general((wk * vc).astype(BF16), kb, TN_DIMS,
                                                   preferred_element_type=F32)
        n_ref[h:h + 1, :] = decay * n_old + jnp.sum(wk * kc, axis=0, keepdims=True)
        m_ref[h:h + 1, :] = m_new


def _gate_select_matrices():
    shape = (N_HEADS, 2 * LANES, 2 * LANES)
    h = lax.broadcasted_iota(jnp.int32, shape, 0)
    r = lax.broadcasted_iota(jnp.int32, shape, 1)
    c = lax.broadcasted_iota(jnp.int32, shape, 2)
    return jnp.where(c < LANES, r == h, r == LANES + N_HEADS + h).astype(BF16)


def _ml_prompt(ml, gates, gates_t, g_mh, chunk):
    t = ml.shape[0]
    tri = jnp.tril(jnp.ones((chunk, chunk), F32)).astype(BF16)
    col = lambda c: pl.BlockSpec((chunk, WIDTH), lambda s, c=c: (s, c))
    return pl.pallas_call(
        functools.partial(_ml_prompt_kernel, chunk=chunk),
        grid=(t // chunk,),
        in_specs=[col(0), col(1), col(2), col(3),
                  pl.BlockSpec((chunk, LANES), lambda s: (s, 0)),
                  pl.BlockSpec((2 * N_HEADS, chunk), lambda s: (0, s)),
                  pl.BlockSpec((1, HEAD_DIM), lambda s: (0, 0)),
                  pl.BlockSpec((chunk, chunk), lambda s: (0, 0)),
                  pl.BlockSpec((chunk, chunk), lambda s: (0, 0)),
                  pl.BlockSpec((N_HEADS, 2 * LANES, 2 * LANES), lambda s: (0, 0, 0))],
        out_specs=[pl.BlockSpec((chunk, WIDTH), lambda s: (s, 0)),
                   pl.BlockSpec((N_HEADS, HEAD_DIM, HEAD_DIM), lambda s: (0, 0, 0)),
                   pl.BlockSpec((N_HEADS, HEAD_DIM), lambda s: (0, 0)),
                   pl.BlockSpec((N_HEADS, LANES), lambda s: (0, 0))],
        out_shape=[jax.ShapeDtypeStruct((t, WIDTH), BF16),
                   jax.ShapeDtypeStruct((N_HEADS, HEAD_DIM, HEAD_DIM), F32),
                   jax.ShapeDtypeStruct((N_HEADS, HEAD_DIM), F32),
                   jax.ShapeDtypeStruct((N_HEADS, LANES), F32)],
        compiler_params=_params(1),
        name="mlstm_prompt",
    )(ml, ml, ml, ml, gates, gates_t, g_mh.reshape(1, HEAD_DIM), tri, tri.T, _gate_select_matrices())


def _ml_sample_kernel(x_ref, g_ref, m_ref, c_ref, n_ref, gmh_ref, hm_ref, cn_ref, nn_ref, mn_ref):
    for b in range(x_ref.shape[0]):
        _ml_sample_one(b, x_ref, g_ref, m_ref, c_ref, n_ref, gmh_ref, hm_ref, cn_ref, nn_ref, mn_ref)


def _ml_sample_one(b, x_ref, g_ref, m_ref, c_ref, n_ref, gmh_ref, hm_ref, cn_ref, nn_ref, mn_ref):
    x = x_ref[b]
    q = x[0:N_HEADS]
    k = x[N_HEADS:2 * N_HEADS] * QK_SCALE
    v = x[2 * N_HEADS:3 * N_HEADS]
    og = x[3 * N_HEADS:4 * N_HEADS]
    gates = g_ref[b]
    li = gates[0:N_HEADS]
    lf = _log_sigmoid(gates[N_HEADS:2 * N_HEADS])
    m_old = m_ref[b]
    n_old = n_ref[b]

    inter = lf + m_old
    m_new = jnp.maximum(inter, li)
    s = jnp.sum(q * k, axis=-1, keepdims=True) * jnp.exp(li - m_new)
    w_inter = jnp.exp(inter - m_new)
    qb = q.astype(BF16)
    rowh = lax.broadcasted_iota(jnp.int32, (N_HEADS, HEAD_DIM), 0)
    cq = jnp.zeros((N_HEADS, HEAD_DIM), F32)
    for h in range(N_HEADS):
        r = lax.dot_general(qb, c_ref[b, h].astype(BF16), NT_DIMS, preferred_element_type=F32)
        cq = jnp.where(rowh == h, r, cq)
    num = s * v + w_inter * cq
    den = s + w_inter * jnp.sum(n_old * q, axis=-1, keepdims=True)
    hh = num / jnp.maximum(jnp.abs(den), jnp.exp(-m_new))
    hm_ref[b] = (_rms(hh, gmh_ref[...]) * _sigmoid(og)).astype(hm_ref.dtype)

    wk = jnp.exp(li - m_new)
    decay = w_inter
    nn_ref[b] = decay * n_old + wk * k
    mn_ref[b] = m_new
    wv = wk * v
    wvt = jnp.concatenate([wv, jnp.zeros((LANES - N_HEADS, HEAD_DIM), F32)], axis=0).T
    for h in range(N_HEADS):
        cn_ref[b, h] = decay[h:h + 1, :] * c_ref[b, h] + wvt[:, h:h + 1] * k[h:h + 1, :]


def _ml_sample(ml, gates, state_m, state_c, state_n, g_mh, group):
    nb = ml.shape[0]
    x = ml.reshape(nb, 4 * N_HEADS, HEAD_DIM)
    g = gates.reshape(nb, 2 * N_HEADS, 1)
    m = state_m.reshape(nb, N_HEADS, 1)
    b3 = lambda *shape: pl.BlockSpec((group,) + shape, lambda b: (b,) + (0,) * len(shape))
    return pl.pallas_call(
        _ml_sample_kernel,
        grid=(nb // group,),
        in_specs=[b3(4 * N_HEADS, HEAD_DIM), b3(2 * N_HEADS, 1), b3(N_HEADS, 1),
                  b3(N_HEADS, HEAD_DIM, HEAD_DIM), b3(N_HEADS, HEAD_DIM),
                  pl.BlockSpec((1, HEAD_DIM), lambda b: (0, 0))],
        out_specs=[b3(N_HEADS, HEAD_DIM), b3(N_HEADS, HEAD_DIM, HEAD_DIM), b3(N_HEADS, HEAD_DIM), b3(N_HEADS, 1)],
        out_shape=[jax.ShapeDtypeStruct((nb, N_HEADS, HEAD_DIM), BF16),
                   jax.ShapeDtypeStruct((nb, N_HEADS, HEAD_DIM, HEAD_DIM), F32),
                   jax.ShapeDtypeStruct((nb, N_HEADS, HEAD_DIM), F32),
                   jax.ShapeDtypeStruct((nb, N_HEADS, 1), F32)],
        compiler_params=_params(1),
        name="mlstm_sample",
    )(x, g, m, state_c, state_n, g_mh.reshape(1, HEAD_DIM))


def _dense_front(x, w, tm, q_scale, q_dtype, kv_dtypes):
    xn, gates = _norm_gate(x, w["g_mix"], w["wg_hi"], w["wg_lo"], w["bg"], min(tm, 512))
    d = x.shape[1]
    gq = (w["g_q"].reshape(1, HEAD_DIM), (1, HEAD_DIM), lambda i, j: (0, 0))
    gk = (w["g_k"].reshape(1, HEAD_DIM), (1, HEAD_DIM), lambda i, j: (0, 0))
    tn = 1024
    w_in_t = w["w_in_t"]
    (q,) = _matmul("proj_q", [xn], [(w_in_t, d, 0, 0)], [gq],
                   functools.partial(_ep_headnorm, scale=q_scale), [q_dtype], WIDTH, tm, tn, True)
    k = _matmul("proj_k", [xn], [(w_in_t, d, 0, WIDTH // tn)], [gk],
                functools.partial(_ep_headnorm, scale=1.0), kv_dtypes, WIDTH, tm, tn, True)
    v = _matmul("proj_v", [xn], [(w_in_t, d, 0, 2 * WIDTH // tn)], [], _ep_plain, kv_dtypes, WIDTH, tm, tn, True)
    (ml,) = _matmul("proj_ml", [xn], [(w_in_t, d, 0, 3 * WIDTH // tn)], [], _ep_plain, [F32], 4 * WIDTH, tm, tn,
                    True)
    return q, k, v, ml, gates


def _dense_back(x, sb, hm, p, w, tm):
    m, d = x.shape
    tn = 1024
    res = lambda arr: (arr, (tm, tn), lambda i, j: (i, j))
    (h1,) = _matmul("out_proj", [sb, hm], [(w["w_out_bf16"], WIDTH, 0, 0), (w["w_out_bf16"], WIDTH, 1, 0)],
                    [res(x)], _ep_residual, [F32], d, tm, tn, w_resident=False)
    hn = _norm(h1, w["g_ffn"], min(tm, 512))
    d_ff = w["w_up"].shape[1]
    (act,) = _matmul("ffn_up", [hn], [(w["w_up"], d, 0, 0)], [], _ep_relu2, [BF16], d_ff, tm, tn)
    tm2, tn2 = tm, 256
    res2 = lambda arr: (arr, (tm2, tn2), lambda i, j: (i, j))
    (h2,) = _matmul("ffn_down", [act], [(w["w_down_bf16"], d_ff, 0, 0)], [res2(h1)], _ep_residual, [F32], d,
                    tm2, tn2, w_resident=False)
    hn2 = _norm(h2, w["g_ple"], min(tm, 512))
    ple = p.shape[1]
    extras = [(p, (tm, ple), lambda i, j: (i, 0)),
              (w["w_ple_bf16"], (ple, tn), lambda i, j: (0, j)),
              res(h2)]
    (y,) = _matmul("ple_gate", [hn2], [(w["w_pg_bf16"], d, 0, 0)], extras, _ep_ple, [F32], d, tm, tn,
                   w_resident=False)
    return y


def kernel(x_prompt, x_sample, cache_k, cache_v, state_C, state_n, state_m, page_table, p_prompt, p_sample,
           g_mix, w_in, b_gate, b_sb, g_q, g_k, g_mh, w_out, g_ffn, w_up, w_down, g_ple, w_ple, w_pg):
    depth = w_in.shape[0]
    assert depth == 1 and x_prompt.shape[0] == 1 and x_sample.shape[1] == 1
    t, d = x_prompt.shape[1], x_prompt.shape[2]
    nb = x_sample.shape[0]
    n_proj = w_in.shape[2] - 2 * N_HEADS

    wg = jnp.pad(w_in[0, :, n_proj:], ((0, 0), (0, LANES - 2 * N_HEADS)))
    wg_hi = wg.astype(BF16)
    w = {
        "g_mix": g_mix[0], "g_q": g_q[0], "g_k": g_k[0], "g_ffn": g_ffn[0], "g_ple": g_ple[0],
        "wg_hi": wg_hi, "wg_lo": (wg - wg_hi.astype(F32)).astype(BF16),
        "bg": jnp.pad(b_gate[0], (0, LANES - 2 * N_HEADS)).reshape(1, LANES),
        "w_in_t": jnp.swapaxes(w_in[0], 0, 1), "w_up": w_up[0],
        "w_out_bf16": w_out[0].astype(BF16), "w_down_bf16": w_down[0].astype(BF16),
        "w_pg_bf16": w_pg[0].astype(BF16), "w_ple_bf16": w_ple[0].astype(BF16),
    }

    xp = x_prompt[0]
    xs = x_sample[:, 0]
    q, (k, k_bf), (v, v_bf), ml, gates = _dense_front(xp, w, 1024, QK_SCALE * LOG2E, BF16, [F32, BF16])
    qs, (ks,), (vs,), mls, gates_s = _dense_front(xs, w, nb, 1.0, F32, [F32])
    sb, sbs = _sb_attention(q, k_bf, v_bf, b_sb[0], 512, qs.reshape(nb, N_HEADS, HEAD_DIM),
                            cache_k, cache_v, page_table)
    chunk = 512
    gates_t = gates[:, :2 * N_HEADS].T
    hm, c_p, n_p, m_p = _ml_prompt(ml, gates, gates_t, g_mh[0], chunk)
    y_p = _dense_back(xp, sb, hm, p_prompt[0, 0], w, 1024)
    hms, c_s, n_s, m_s = _ml_sample(mls, gates_s[:, :2 * N_HEADS], state_m[0], state_C[0], state_n[0], g_mh[0], 8)
    y_s = _dense_back(xs, sbs.reshape(nb, WIDTH), hms.reshape(nb, WIDTH), p_sample[0, :, 0], w, nb)

    hd = (N_HEADS, HEAD_DIM)
    return (y_p[None], y_s[:, None],
            k.reshape((1, 1, t) + hd), v.reshape((1, 1, t) + hd),
            c_p[None, None], n_p[None, None], m_p[:, 0][None, None],
            ks.reshape((1, nb, 1) + hd), vs.reshape((1, nb, 1) + hd),
            c_s[None], n_s[None], m_s.reshape(1, nb, N_HEADS))
```
